```python
import math
import jax, jax.numpy as jnp
from jax import lax
import numpy as np

D_MODEL = 1024
BATCH = 8
SEQ = 4096
DEPTH = 4

N_EVEN = (DEPTH + 1) // 2
N_ODD = DEPTH // 2
EPS = 1e-6

GM_GROUPS = 4
GM_CH = D_MODEL // 8
GM_WIDTH = GM_GROUPS * GM_CH
GM_CHUNK = 128
POOL_WINDOWS = (2, 4, 8, 16)
POOL_GROUPS = len(POOL_WINDOWS)
POOL_CH = D_MODEL // 8
POOL_WIDTH = POOL_GROUPS * POOL_CH
POOL_MAXW = max(POOL_WINDOWS)
EVEN_IN = 2 * GM_WIDTH + POOL_WIDTH
EVEN_MIX = GM_WIDTH + POOL_WIDTH

DA_HEADS = D_MODEL // 128
DA_QK_DIM = 64
DA_V_DIM = 2 * DA_QK_DIM
DA_Q_WIDTH = DA_HEADS * 2 * DA_QK_DIM
DA_V_WIDTH = DA_HEADS * DA_V_DIM
ODD_IN = 2 * DA_Q_WIDTH + DA_V_WIDTH
Q_BLOCK = 128
ROPE_THETA = 10000.0

PEER_HEADS = 8
PEER_NKEYS = 128
PEER_EXPERTS = PEER_NKEYS * PEER_NKEYS
PEER_QDIM = 256
PEER_HALF = PEER_QDIM // 2
PEER_TOPK = 16
PEER_TOKEN_BLOCK = 128

kernel_name = "hybrid_gmlp_pool_diffattn_peer_adaln"


def rmsnorm(x, g):
    xf = x.astype(jnp.float32)
    y = xf * lax.rsqrt(jnp.mean(xf * xf, axis=-1, keepdims=True) + EPS) * g
    return y.astype(x.dtype)


def apply_rope(t, cos, sin):
    t1, t2 = jnp.split(t, 2, axis=-1)
    return jnp.concatenate([t1 * cos - t2 * sin, t2 * cos + t1 * sin], axis=-1)


def chunked_spatial_gating(u, v, w_s, b_s, g_v):
    B, S, _ = v.shape
    vf = v.astype(jnp.float32)
    mu = jnp.mean(vf, axis=-1, keepdims=True)
    var = jnp.mean((vf - mu) ** 2, axis=-1, keepdims=True)
    vn = ((vf - mu) * lax.rsqrt(var + EPS) * g_v).astype(v.dtype)
    vn = vn.reshape(B, S // GM_CHUNK, GM_CHUNK, GM_GROUPS, GM_CH)
    causal = jnp.tril(jnp.ones((GM_CHUNK, GM_CHUNK), dtype=bool))
    w = jnp.where(causal[None], w_s, 0)
    sv = jnp.einsum('gts,bnsgc->bntgc', w, vn) + b_s.T[None, None, :, :, None]
    return u * sv.reshape(B, S, GM_WIDTH)


def multiscale_pool(p, w_pool, ls):
    B, S, _ = p.shape
    pf = p.astype(jnp.float32)
    cs = jnp.cumsum(pf, axis=1)
    cs_pad = jnp.pad(cs, ((0, 0), (POOL_MAXW, 0), (0, 0)))
    t = jnp.arange(S)
    outs = []
    for g, w in enumerate(POOL_WINDOWS):
        sl = slice(g * POOL_CH, (g + 1) * POOL_CH)
        lag = cs_pad[:, POOL_MAXW - w:POOL_MAXW - w + S, sl]
        cnt = jnp.minimum(t + 1, w).astype(jnp.float32)[None, :, None]
        outs.append((cs[..., sl] - lag) / cnt - pf[..., sl])
    pooled = jnp.stack(outs, axis=2).astype(p.dtype)
    mixed = jnp.einsum('bsgc,gcd->bsgd', pooled, w_pool).reshape(B, S, POOL_WIDTH)
    return mixed * ls


def diff_attention(h, w_in, lam_q1, lam_k1, lam_q2, lam_k2, g_sub, w_out, cos, sin, lam_init):
    B, S, _ = h.shape
    proj = h @ w_in
    q = proj[..., :DA_Q_WIDTH].reshape(B, S, DA_HEADS, 2, DA_QK_DIM)
    k = proj[..., DA_Q_WIDTH:2 * DA_Q_WIDTH].reshape(B, S, DA_HEADS, 2, DA_QK_DIM)
    v = proj[..., 2 * DA_Q_WIDTH:].reshape(B, S, DA_HEADS, DA_V_DIM)
    q = apply_rope(q, cos, sin)
    k = apply_rope(k, cos, sin)
    lam = (jnp.exp(jnp.sum(lam_q1.astype(jnp.float32) * lam_k1.astype(jnp.float32)))
           - jnp.exp(jnp.sum(lam_q2.astype(jnp.float32) * lam_k2.astype(jnp.float32)))
           + lam_init)
    scale = DA_QK_DIM ** -0.5
    outs = []
    for i in range(S // Q_BLOCK):
        kv_len = (i + 1) * Q_BLOCK
        q_i = q[:, i * Q_BLOCK:kv_len]
        s = jnp.einsum('bqhmd,bkhmd->bhmqk', q_i, k[:, :kv_len]).astype(jnp.float32) * scale
        q_pos = i * Q_BLOCK + jnp.arange(Q_BLOCK)
        mask = jnp.arange(kv_len)[None, :] <= q_pos[:, None]
        s = jnp.where(mask, s, -jnp.inf)
        pr = jax.nn.softmax(s, axis=-1)
        a = pr[:, :, 0] - lam * pr[:, :, 1]
        outs.append(jnp.einsum('bhqk,bkhd->bqhd', a.astype(v.dtype), v[:, :kv_len]))
    o = jnp.concatenate(outs, axis=1).astype(jnp.float32)
    o = o * lax.rsqrt(jnp.mean(o * o, axis=-1, keepdims=True) + EPS) * g_sub * (1.0 - lam_init)
    return o.astype(h.dtype).reshape(B, S, DA_V_WIDTH) @ w_out


def peer(h, w_q, sub_keys, u_emb, v_emb):
    B, S, D = h.shape
    T = PEER_TOKEN_BLOCK
    K = PEER_TOPK
    hb = h.reshape(B * S // T, T, D)

    def block(x_t):
        q = jnp.einsum('td,dhk->thk', x_t, w_q).reshape(T, PEER_HEADS, 2, PEER_HALF)
        s = jnp.einsum('thpk,hpnk->thpn', q, sub_keys).astype(jnp.float32)
        sv, si = lax.top_k(s, K)
        cand = (sv[:, :, 0, :, None] + sv[:, :, 1, None, :]).reshape(T, PEER_HEADS, K * K)
        cidx = (si[:, :, 0, :, None] * PEER_NKEYS + si[:, :, 1, None, :]).reshape(T, PEER_HEADS, K * K)
        fv, fpos = lax.top_k(cand, K)
        eidx = jnp.take_along_axis(cidx, fpos, axis=-1)
        gate = jax.nn.softmax(fv, axis=-1)
        u = u_emb[eidx]
        v = v_emb[eidx]
        act = jax.nn.gelu(jnp.einsum('thkd,td->thk', u, x_t).astype(jnp.float32), approximate=False)
        return jnp.einsum('thk,thkd->td', (gate * act).astype(x_t.dtype), v)

    return lax.map(block, hb).reshape(B, S, D)


def setup_inputs(seed: int = 0) -> dict:
    key = jax.random.key(seed)
    ks = jax.random.split(key, 32)
    f32 = jnp.float32
    D = D_MODEL

    def nrm(k, shape, scale):
        return jax.random.normal(k, shape, f32) * scale

    def gain(k, shape):
        return 1.0 + 0.02 * jax.random.normal(k, shape, f32)

    return {
        "x": nrm(ks[0], (BATCH, SEQ, D), 1.0),
        "c": nrm(ks[1], (BATCH, D), 1.0),
        "positions": jnp.tile(jnp.arange(SEQ, dtype=jnp.int32)[None, :], (BATCH, 1)),
        "ada_w": nrm(ks[2], (DEPTH, D, 6 * D), 0.5 * D ** -0.5),
        "ada_b": nrm(ks[3], (DEPTH, 6 * D), 0.02),
        "norm_mix": gain(ks[4], (DEPTH, D)),
        "norm_ffn": gain(ks[5], (DEPTH, D)),
        "ev_w_in": nrm(ks[6], (N_EVEN, D, EVEN_IN), D ** -0.5),
        "ev_g_v": gain(ks[7], (N_EVEN, GM_WIDTH)),
        "ev_w_s": nrm(ks[8], (N_EVEN, GM_GROUPS, GM_CHUNK, GM_CHUNK), GM_CHUNK ** -0.5),
        "ev_b_s": gain(ks[9], (N_EVEN, GM_GROUPS, GM_CHUNK)),
        "ev_w_pool": nrm(ks[10], (N_EVEN, POOL_GROUPS, POOL_CH, POOL_CH), POOL_CH ** -0.5),
        "ev_pool_scale": gain(ks[11], (N_EVEN, POOL_WIDTH)),
        "ev_w_out": nrm(ks[12], (N_EVEN, EVEN_MIX, D), EVEN_MIX ** -0.5),
        "od_w_in": nrm(ks[13], (N_ODD, D, ODD_IN), D ** -0.5),
        "od_lam_q1": nrm(ks[14], (N_ODD, DA_QK_DIM), 0.1),
        "od_lam_k1": nrm(ks[15], (N_ODD, DA_QK_DIM), 0.1),
        "od_lam_q2": nrm(ks[16], (N_ODD, DA_QK_DIM), 0.1),
        "od_lam_k2": nrm(ks[17], (N_ODD, DA_QK_DIM), 0.1),
        "od_g_sub": gain(ks[18], (N_ODD, DA_V_DIM)),
        "od_w_out": nrm(ks[19], (N_ODD, DA_V_WIDTH, D), DA_V_WIDTH ** -0.5),
        "peer_w_q": nrm(ks[20], (DEPTH, D, PEER_HEADS, PEER_QDIM), D ** -0.5),
        "peer_sub_keys": nrm(ks[21], (DEPTH, PEER_HEADS, 2, PEER_NKEYS, PEER_HALF), PEER_HALF ** -0.5),
        "peer_u": nrm(ks[22], (DEPTH, PEER_EXPERTS, D), D ** -0.5),
        "peer_v": nrm(ks[23], (DEPTH, PEER_EXPERTS, D), PEER_HEADS ** -0.5),
        "final_norm": gain(ks[24], (D,)),
    }


def reference(x, c, positions, ada_w, ada_b, norm_mix, norm_ffn,
              ev_w_in, ev_g_v, ev_w_s, ev_b_s, ev_w_pool, ev_pool_scale, ev_w_out,
              od_w_in, od_lam_q1, od_lam_k1, od_lam_q2, od_lam_k2, od_g_sub, od_w_out,
              peer_w_q, peer_sub_keys, peer_u, peer_v, final_norm):
    B, S, D = x.shape
    inv_freq = 1.0 / (ROPE_THETA ** (jnp.arange(0, DA_QK_DIM, 2, dtype=jnp.float32) / DA_QK_DIM))
    ang = positions.astype(jnp.float32)[..., None] * inv_freq
    cos = jnp.cos(ang)[:, :, None, None, :].astype(x.dtype)
    sin = jnp.sin(ang)[:, :, None, None, :].astype(x.dtype)
    c_act = jax.nn.silu(c)
    for l in range(DEPTH):
        mod = (c_act @ ada_w[l] + ada_b[l]).reshape(B, 6, D)[:, :, None, :]
        sh1, sc1, g1, sh2, sc2, g2 = (mod[:, i] for i in range(6))
        h = rmsnorm(x, norm_mix[l]) * (1 + sc1) + sh1
        if l % 2 == 0:
            e = l // 2
            proj = h @ ev_w_in[e]
            z = jax.nn.gelu(proj[..., :2 * GM_WIDTH], approximate=False)
            ya = chunked_spatial_gating(z[..., :GM_WIDTH], z[..., GM_WIDTH:],
                                        ev_w_s[e], ev_b_s[e], ev_g_v[e])
            yb = multiscale_pool(proj[..., 2 * GM_WIDTH:], ev_w_pool[e], ev_pool_scale[e])
            y = jnp.concatenate([ya, yb], axis=-1) @ ev_w_out[e]
        else:
            o = l // 2
            lam_init = 0.8 - 0.6 * math.exp(-0.3 * l)
            y = diff_attention(h, od_w_in[o], od_lam_q1[o], od_lam_k1[o], od_lam_q2[o],
                               od_lam_k2[o], od_g_sub[o], od_w_out[o], cos, sin, lam_init)
        x = x + g1 * y
        h = rmsnorm(x, norm_ffn[l]) * (1 + sc2) + sh2
        x = x + g2 * peer(h, peer_w_q[l], peer_sub_keys[l], peer_u[l], peer_v[l])
    return rmsnorm(x, final_norm)
```

```python
import functools
import math

import jax
import jax.numpy as jnp
from jax import lax
from jax.experimental import pallas as pl
from jax.experimental.pallas import tpu as pltpu

F32 = jnp.float32
BF16 = jnp.bfloat16
EPS = 1e-6
NEG_INF = float("-inf")

GM_GROUPS = 4
GM_CH = 128
GM_WIDTH = GM_GROUPS * GM_CH
GM_CHUNK = 128
POOL_WINDOWS = (2, 4, 8, 16)
POOL_CH = 128
POOL_WIDTH = len(POOL_WINDOWS) * POOL_CH
POOL_HALO = max(POOL_WINDOWS)
DA_HEADS = 8
DA_QK = 64
DA_V = 128
ROPE_THETA = 10000.0
PEER_HEADS = 8
PEER_NKEYS = 128
PEER_TOPK = 16

V7X_VMEM_LIMIT = 56 * 1024 * 1024

EVEN_TS = 512
QKV_TS = 512
ATT_TQ = 256
ATT_TK = 256
ROUTE_T = 256
DENSE_T = 512
DENSE_EC = 1024


def _params(sem):
    return pltpu.CompilerParams(dimension_semantics=sem, vmem_limit_bytes=V7X_VMEM_LIMIT)


def _gelu(x):
    return 0.5 * x * (1.0 + lax.erf(x * 0.7071067811865476))


def _norm_mod(x, gn, sc, sh):
    ms = jnp.mean(x * x, axis=-1, keepdims=True)
    return x * lax.rsqrt(ms + EPS) * gn * (1.0 + sc) + sh


def _ada_kernel(c_ref, w_ref, b_ref, o_ref):
    c = c_ref[...]
    ca = c * jax.nn.sigmoid(c)
    o_ref[0, 0] = jnp.dot(ca, w_ref[0], preferred_element_type=F32,
                          precision=lax.Precision.HIGHEST) + b_ref[0, 0]


def _ada(c, ada_w, ada_b):
    L, D, _ = ada_w.shape
    B = c.shape[0]
    out = pl.pallas_call(
        _ada_kernel,
        out_shape=jax.ShapeDtypeStruct((L, 6, B, D), F32),
        grid=(L, 6),
        in_specs=[
            pl.BlockSpec((B, D), lambda l, j: (0, 0)),
            pl.BlockSpec((1, D, D), lambda l, j: (l, 0, j)),
            pl.BlockSpec((1, 1, 1, D), lambda l, j: (l, j, 0, 0)),
        ],
        out_specs=pl.BlockSpec((1, 1, B, D), lambda l, j: (l, j, 0, 0)),
        compiler_params=_params(("arbitrary", "arbitrary")),
        name="ada",
    )(c, ada_w, ada_b.reshape(L, 6, 1, D))
    return jnp.swapaxes(out, 1, 2)


def _even_kernel(x_ref, mod_ref, gn_ref, win_ref, gv_ref, ws_ref, bs_ref, wp_ref, ls_ref,
                 wout_ref, o_ref, ext_ref, mix_ref, *, ts):
    si = pl.program_id(1)
    x = x_ref[0]
    sh1 = mod_ref[0, 0:1, :]
    sc1 = mod_ref[0, 1:2, :]
    g1 = mod_ref[0, 2:3, :]
    h = _norm_mod(x, gn_ref[...], sc1, sh1)
    proj = jnp.dot(h.astype(BF16), win_ref[...], preferred_element_type=F32)

    z = _gelu(proj[:, :2 * GM_WIDTH])
    u = z[:, :GM_WIDTH]
    v = z[:, GM_WIDTH:]
    mu = jnp.mean(v, axis=-1, keepdims=True)
    d = v - mu
    var = jnp.mean(d * d, axis=-1, keepdims=True)
    vn = (d * lax.rsqrt(var + EPS) * gv_ref[...]).astype(BF16)
    r = lax.broadcasted_iota(jnp.int32, (GM_CHUNK, GM_CHUNK), 0)
    cidx = lax.broadcasted_iota(jnp.int32, (GM_CHUNK, GM_CHUNK), 1)
    causal = r >= cidx
    for g in range(GM_GROUPS):
        wg = jnp.where(causal, ws_ref[g], 0.0).astype(BF16)
        for c in range(ts // GM_CHUNK):
            rs = slice(c * GM_CHUNK, (c + 1) * GM_CHUNK)
            cs = slice(g * GM_CH, (g + 1) * GM_CH)
            sv = jnp.dot(wg, vn[rs, cs], preferred_element_type=F32) + bs_ref[g]
            mix_ref[rs, cs] = (u[rs, cs] * sv).astype(BF16)

    p = proj[:, 2 * GM_WIDTH:]

    @pl.when(si == 0)
    def _():
        ext_ref[0:POOL_HALO, :] = jnp.zeros((POOL_HALO, POOL_WIDTH), F32)

    ext_ref[POOL_HALO:POOL_HALO + ts, :] = p
    t = si * ts + lax.broadcasted_iota(jnp.int32, (ts, POOL_CH), 0)
    for g, w in enumerate(POOL_WINDOWS):
        cs = slice(g * POOL_CH, (g + 1) * POOL_CH)
        acc = ext_ref[POOL_HALO:POOL_HALO + ts, cs]
        for k in range(1, w):
            acc = acc + ext_ref[POOL_HALO - k:POOL_HALO - k + ts, cs]
        cnt = jnp.minimum(t + 1, w).astype(F32)
        pooled = acc / cnt - p[:, cs]
        yb = jnp.dot(pooled.astype(BF16), wp_ref[g], preferred_element_type=F32) * ls_ref[:, cs]
        mix_ref[:, GM_WIDTH + g * POOL_CH:GM_WIDTH + (g + 1) * POOL_CH] = yb.astype(BF16)
    ext_ref[0:POOL_HALO, :] = ext_ref[ts:ts + POOL_HALO, :]

    y = jnp.dot(mix_ref[...], wout_ref[...], preferred_element_type=F32)
    o_ref[0] = x + g1 * y


def _even_layer(x, mod, gn, w_in, g_v, w_s, b_s, w_pool, ls, w_out):
    B, S, D = x.shape
    ts = EVEN_TS
    n_in = w_in.shape[1]
    bsb = jnp.broadcast_to(b_s[:, :, None], (GM_GROUPS, GM_CHUNK, GM_CH))
    return pl.pallas_call(
        functools.partial(_even_kernel, ts=ts),
        out_shape=jax.ShapeDtypeStruct((B, S, D), F32),
        grid=(B, S // ts),
        in_specs=[
            pl.BlockSpec((1, ts, D), lambda b, s: (b, s, 0)),
            pl.BlockSpec((1, 6, D), lambda b, s: (b, 0, 0)),
            pl.BlockSpec((1, D), lambda b, s: (0, 0)),
            pl.BlockSpec((D, n_in), lambda b, s: (0, 0)),
            pl.BlockSpec((1, GM_WIDTH), lambda b, s: (0, 0)),
            pl.BlockSpec((GM_GROUPS, GM_CHUNK, GM_CHUNK), lambda b, s: (0, 0, 0)),
            pl.BlockSpec((GM_GROUPS, GM_CHUNK, GM_CH), lambda b, s: (0, 0, 0)),
            pl.BlockSpec((len(POOL_WINDOWS), POOL_CH, POOL_CH), lambda b, s: (0, 0, 0)),
            pl.BlockSpec((1, POOL_WIDTH), lambda b, s: (0, 0)),
            pl.BlockSpec((GM_WIDTH + POOL_WIDTH, D), lambda b, s: (0, 0)),
        ],
        out_specs=pl.BlockSpec((1, ts, D), lambda b, s: (b, s, 0)),
        scratch_shapes=[
            pltpu.VMEM((POOL_HALO + ts, POOL_WIDTH), F32),
            pltpu.VMEM((ts, GM_WIDTH + POOL_WIDTH), BF16),
        ],
        compiler_params=_params(("arbitrary", "arbitrary")),
        name="even_mixer",
    )(x, mod, gn.reshape(1, D), w_in.astype(BF16), g_v.reshape(1, -1), w_s, bsb,
      w_pool.astype(BF16), ls.reshape(1, -1), w_out.astype(BF16))


def _qkv_kernel(x_ref, mod_ref, gn_ref, wt_ref, cos_ref, sin_ref, qt_ref, k_ref, vt_ref,
                kt_scr, *, d_model):
    x = x_ref[0]
    sh1 = mod_ref[0, 0:1, :]
    sc1 = mod_ref[0, 1:2, :]
    h = _norm_mod(x, gn_ref[...], sc1, sh1)
    ht = h.T.astype(BF16)
    pt = jnp.dot(wt_ref[...], ht, preferred_element_type=F32)
    cos = cos_ref[0]
    sin = sin_ref[0]
    half = DA_QK // 2
    scale = DA_QK ** -0.5
    for g in range(2 * DA_HEADS):
        r0 = g * DA_QK
        t1 = pt[r0:r0 + half]
        t2 = pt[r0 + half:r0 + DA_QK]
        qt_ref[0, r0:r0 + half, :] = ((t1 * cos - t2 * sin) * scale).astype(BF16)
        qt_ref[0, r0 + half:r0 + DA_QK, :] = ((t2 * cos + t1 * sin) * scale).astype(BF16)
        k0 = d_model + r0
        t1 = pt[k0:k0 + half]
        t2 = pt[k0 + half:k0 + DA_QK]
        kt_scr[r0:r0 + half, :] = t1 * cos - t2 * sin
        kt_scr[r0 + half:r0 + DA_QK, :] = t2 * cos + t1 * sin
    k_ref[0] = kt_scr[...].T.astype(BF16)
    vt_ref[0] = pt[2 * d_model:3 * d_model].astype(BF16)


def _qkv(x, mod, gn, w_in, cos_t, sin_t):
    B, S, D = x.shape
    ts = QKV_TS
    wt = w_in.T.astype(BF16)
    return pl.pallas_call(
        functools.partial(_qkv_kernel, d_model=D),
        out_shape=(
            jax.ShapeDtypeStruct((B, D, S), BF16),
            jax.ShapeDtypeStruct((B, S, D), BF16),
            jax.ShapeDtypeStruct((B, D, S), BF16),
        ),
        grid=(B, S // ts),
        in_specs=[
            pl.BlockSpec((1, ts, D), lambda b, s: (b, s, 0)),
            pl.BlockSpec((1, 6, D), lambda b, s: (b, 0, 0)),
            pl.BlockSpec((1, D), lambda b, s: (0, 0)),
            pl.BlockSpec((3 * D, D), lambda b, s: (0, 0)),
            pl.BlockSpec((1, DA_QK // 2, ts), lambda b, s: (b, 0, s)),
            pl.BlockSpec((1, DA_QK // 2, ts), lambda b, s: (b, 0, s)),
        ],
        out_specs=(
            pl.BlockSpec((1, D, ts), lambda b, s: (b, 0, s)),
            pl.BlockSpec((1, ts, D), lambda b, s: (b, s, 0)),
            pl.BlockSpec((1, D, ts), lambda b, s: (b, 0, s)),
        ),
        scratch_shapes=[pltpu.VMEM((D, ts), F32)],
        compiler_params=_params(("arbitrary", "arbitrary")),
        name="qkv_rope",
    )(x, mod, gn.reshape(1, D), wt, cos_t, sin_t)


def _attn_kernel(qt_ref, k_ref, vt_ref, lam_ref, gs_ref, o_ref, m_scr, l_scr, acc_scr,
                 *, tq, tk, lam_init):
    qi = pl.program_id(2)
    qt = qt_ref[0]
    row = lax.broadcasted_iota(jnp.int32, qt.shape, 0)
    zero = jnp.zeros_like(qt)
    qs = (jnp.where(row < DA_QK, qt, zero), jnp.where(row >= DA_QK, qt, zero))

    m_scr[...] = jnp.full(m_scr.shape, NEG_INF, F32)
    l_scr[...] = jnp.zeros(l_scr.shape, F32)
    acc_scr[...] = jnp.zeros(acc_scr.shape, F32)

    def update(j, masked):
        off = pl.multiple_of(j * tk, tk)
        kb = k_ref[0, pl.ds(off, tk), :]
        vb = vt_ref[0, :, pl.ds(off, tk)]
        for a in range(2):
            s = jnp.dot(kb, qs[a], preferred_element_type=F32)
            if masked:
                kpos = lax.broadcasted_iota(jnp.int32, s.shape, 0)
                qpos = lax.broadcasted_iota(jnp.int32, s.shape, 1)
                s = jnp.where(kpos <= qpos, s, NEG_INF)
            m_old = m_scr[a]
            m_new = jnp.maximum(m_old, jnp.max(s, axis=0, keepdims=True))
            alpha = jnp.exp(m_old - m_new)
            p = jnp.exp(s - m_new)
            l_scr[a] = alpha * l_scr[a] + jnp.sum(p, axis=0, keepdims=True)
            acc_scr[a] = alpha * acc_scr[a] + jnp.dot(vb, p.astype(BF16),
                                                      preferred_element_type=F32)
            m_scr[a] = m_new

    def body(j, carry):
        update(j, False)
        return carry

    lax.fori_loop(0, qi, body, 0)
    update(qi, True)

    lv = lam_ref[...]
    la = jnp.sum(lv[0:1] * lv[1:2], axis=-1, keepdims=True)
    lb = jnp.sum(lv[2:3] * lv[3:4], axis=-1, keepdims=True)
    lam = jnp.exp(la) - jnp.exp(lb) + lam_init
    o = acc_scr[0] / l_scr[0] - lam * (acc_scr[1] / l_scr[1])
    ms = jnp.mean(o * o, axis=0, keepdims=True)
    o = o * lax.rsqrt(ms + EPS) * gs_ref[...] * (1.0 - lam_init)
    o_ref[0] = o.astype(BF16)


def _attention(qt, k, vt, lam_vecs, g_sub, lam_init):
    B, D, S = qt.shape
    tq, tk = ATT_TQ, ATT_TK
    assert tq == tk
    hd = 2 * DA_QK
    return pl.pallas_call(
        functools.partial(_attn_kernel, tq=tq, tk=tk, lam_init=lam_init),
        out_shape=jax.ShapeDtypeStruct((B, D, S), BF16),
        grid=(B, DA_HEADS, S // tq),
        in_specs=[
            pl.BlockSpec((1, hd, tq), lambda b, h, q: (b, h, q)),
            pl.BlockSpec((1, S, hd), lambda b, h, q: (b, 0, h)),
            pl.BlockSpec((1, DA_V, S), lambda b, h, q: (b, h, 0)),
            pl.BlockSpec((4, DA_QK), lambda b, h, q: (0, 0)),
            pl.BlockSpec((DA_V, 1), lambda b, h, q: (0, 0)),
        ],
        out_specs=pl.BlockSpec((1, DA_V, tq), lambda b, h, q: (b, h, q)),
        scratch_shapes=[
            pltpu.VMEM((2, 1, tq), F32),
            pltpu.VMEM((2, 1, tq), F32),
            pltpu.VMEM((2, DA_V, tq), F32),
        ],
        compiler_params=_params(("arbitrary", "arbitrary", "arbitrary")),
        name="diff_attention",
    )(qt, k, vt, lam_vecs, g_sub.reshape(DA_V, 1))


def _oproj_kernel(ot_ref, wt_ref, x_ref, mod_ref, o_ref):
    yt = jnp.dot(wt_ref[...], ot_ref[0], preferred_element_type=F32)
    g1 = mod_ref[0, 2:3, :]
    o_ref[0] = x_ref[0] + g1 * yt.T


def _oproj(ot, w_out, x, mod):
    B, S, D = x.shape
    ts = QKV_TS
    return pl.pallas_call(
        _oproj_kernel,
        out_shape=jax.ShapeDtypeStruct((B, S, D), F32),
        grid=(B, S // ts),
        in_specs=[
            pl.BlockSpec((1, D, ts), lambda b, s: (b, 0, s)),
            pl.BlockSpec((D, D), lambda b, s: (0, 0)),
            pl.BlockSpec((1, ts, D), lambda b, s: (b, s, 0)),
            pl.BlockSpec((1, 6, D), lambda b, s: (b, 0, 0)),
        ],
        out_specs=pl.BlockSpec((1, ts, D), lambda b, s: (b, s, 0)),
        compiler_params=_params(("arbitrary", "arbitrary")),
        name="attn_out_proj",
    )(ot, w_out.T.astype(BF16), x, mod)


def _top_values(s, k):
    vals = []
    for _ in range(k):
        m = jnp.max(s, axis=0, keepdims=True)
        vals.append(m)
        s = jnp.where(s == m, NEG_INF, s)
    return vals


def _route_kernel(x_ref, mod_ref, gn_ref, wq_ref, sk_ref, ht_ref, s1_ref, s2_ref, st_ref):
    x = x_ref[...]
    sh2 = mod_ref[0, 3:4, :]
    sc2 = mod_ref[0, 4:5, :]
    h = _norm_mod(x, gn_ref[...], sc2, sh2)
    ht_ref[...] = h.T.astype(BF16)
    q = jnp.dot(h.astype(BF16), wq_ref[...], preferred_element_type=F32)
    K = PEER_TOPK
    for hd in range(PEER_HEADS):
        tops = []
        for p, s_ref in ((0, s1_ref), (1, s2_ref)):
            c0 = (hd * 2 + p) * PEER_NKEYS
            qhp = q[:, c0:c0 + PEER_NKEYS].astype(BF16)
            s = lax.dot_general(sk_ref[hd * 2 + p], qhp, (((1,), (1,)), ((), ())),
                                preferred_element_type=F32)
            s_ref[hd * PEER_NKEYS:(hd + 1) * PEER_NKEYS, :] = s
            tops.append(_top_values(s, K))
        a, b = tops
        b_lo = jnp.concatenate(b[:8], axis=0)
        b_hi = jnp.concatenate(b[8:], axis=0)
        a_hi = jnp.concatenate(a[8:], axis=0)
        cand = [a[k] + b_lo for k in range(8)] + [a[0] + b_hi, a_hi + b[0]]
        c = _top_values(jnp.concatenate(cand, axis=0), K)
        zsum = jnp.ones_like(c[0])
        for k in range(1, K):
            zsum = zsum + jnp.exp(c[k] - c[0])
        st_ref[hd * 4:(hd + 1) * 4, :] = jnp.concatenate(
            [c[K - 1], a[0], b[0], 1.0 / zsum], axis=0)


def _dense_kernel(ht_ref, u_ref, vt_ref, s1_ref, s2_ref, st_ref, o_ref, e1_scr, e2_scr, wa_scr,
                  *, ec_size):
    ec = pl.program_id(1)
    nk = PEER_NKEYS

    @pl.when(ec == 0)
    def _():
        o_ref[...] = jnp.zeros(o_ref.shape, F32)
        for hd in range(PEER_HEADS):
            rs = slice(hd * nk, (hd + 1) * nk)
            a1 = st_ref[hd * 4 + 1:hd * 4 + 2, :]
            b1 = st_ref[hd * 4 + 2:hd * 4 + 3, :]
            zinv = st_ref[hd * 4 + 3:hd * 4 + 4, :]
            e1_scr[rs, :] = jnp.exp(s1_ref[rs, :] - a1)
            e2_scr[rs, :] = jnp.exp(s2_ref[rs, :] - b1) * zinv

    act = jnp.dot(u_ref[...], ht_ref[...], preferred_element_type=F32)
    per = ec_size // nk
    for ii in range(per):
        i = ec * per + ii
        w = jnp.zeros((nk, act.shape[1]), F32)
        for hd in range(PEER_HEADS):
            rs = slice(hd * nk, (hd + 1) * nk)
            s1row = s1_ref[pl.ds(hd * nk + i, 1), :]
            e1row = e1_scr[pl.ds(hd * nk + i, 1), :]
            tau = st_ref[hd * 4:hd * 4 + 1, :]
            sel = (s1row + s2_ref[rs, :]) >= tau
            w = w + jnp.where(sel, e2_scr[rs, :], 0.0) * e1row
        g = _gelu(act[ii * nk:(ii + 1) * nk, :])
        wa_scr[ii * nk:(ii + 1) * nk, :] = (w * g).astype(BF16)
    o_ref[...] += jnp.dot(vt_ref[...], wa_scr[...], preferred_element_type=F32)


def _peer(x, mod, gn, w_q, sub_keys, u_bf, vt_bf):
    B, S, D = x.shape
    N = B * S
    H, NK = PEER_HEADS, PEER_NKEYS
    E = u_bf.shape[0]
    tr = ROUTE_T
    x2 = x.reshape(N, D)
    wq = w_q.reshape(D, H * 2 * NK).astype(BF16)
    sk = sub_keys.reshape(H * 2, NK, NK).astype(BF16)
    ht, s1, s2, st = pl.pallas_call(
        _route_kernel,
        out_shape=(
            jax.ShapeDtypeStruct((D, N), BF16),
            jax.ShapeDtypeStruct((H * NK, N), F32),
            jax.ShapeDtypeStruct((H * NK, N), F32),
            jax.ShapeDtypeStruct((H * 4, N), F32),
        ),
        grid=(N // tr,),
        in_specs=[
            pl.BlockSpec((tr, D), lambda i: (i, 0)),
            pl.BlockSpec((1, 6, D), lambda i: ((i * tr) // S, 0, 0)),
            pl.BlockSpec((1, D), lambda i: (0, 0)),
            pl.BlockSpec((D, H * 2 * NK), lambda i: (0, 0)),
            pl.BlockSpec((H * 2, NK, NK), lambda i: (0, 0, 0)),
        ],
        out_specs=(
            pl.BlockSpec((D, tr), lambda i: (0, i)),
            pl.BlockSpec((H * NK, tr), lambda i: (0, i)),
            pl.BlockSpec((H * NK, tr), lambda i: (0, i)),
            pl.BlockSpec((H * 4, tr), lambda i: (0, i)),
        ),
        compiler_params=_params(("arbitrary",)),
        name="peer_route",
    )(x2, mod, gn.reshape(1, D), wq, sk)

    td, ec = DENSE_T, DENSE_EC
    return pl.pallas_call(
        functools.partial(_dense_kernel, ec_size=ec),
        out_shape=jax.ShapeDtypeStruct((D, N), F32),
        grid=(N // td, E // ec),
        in_specs=[
            pl.BlockSpec((D, td), lambda t, e: (0, t)),
            pl.BlockSpec((ec, D), lambda t, e: (e, 0)),
            pl.BlockSpec((D, ec), lambda t, e: (0, e)),
            pl.BlockSpec((H * NK, td), lambda t, e: (0, t)),
            pl.BlockSpec((H * NK, td), lambda t, e: (0, t)),
            pl.BlockSpec((H * 4, td), lambda t, e: (0, t)),
        ],
        out_specs=pl.BlockSpec((D, td), lambda t, e: (0, t)),
        scratch_shapes=[
            pltpu.VMEM((H * NK, td), F32),
            pltpu.VMEM((H * NK, td), F32),
            pltpu.VMEM((ec, td), BF16),
        ],
        compiler_params=_params(("arbitrary", "arbitrary")),
        name="peer_experts",
    )(ht, u_bf, vt_bf, s1, s2, st)


def _final_kernel(x_ref, g_ref, o_ref):
    x = x_ref[...]
    ms = jnp.mean(x * x, axis=-1, keepdims=True)
    o_ref[...] = x * lax.rsqrt(ms + EPS) * g_ref[...]


def _final_norm(x, g):
    B, S, D = x.shape
    N = B * S
    t = 1024
    out = pl.pallas_call(
        _final_kernel,
        out_shape=jax.ShapeDtypeStruct((N, D), F32),
        grid=(N // t,),
        in_specs=[pl.BlockSpec((t, D), lambda i: (i, 0)), pl.BlockSpec((1, D), lambda i: (0, 0))],
        out_specs=pl.BlockSpec((t, D), lambda i: (i, 0)),
        compiler_params=_params(("arbitrary",)),
        name="final_norm",
    )(x.reshape(N, D), g.reshape(1, D))
    return out.reshape(B, S, D)


def kernel(x, c, positions, ada_w, ada_b, norm_mix, norm_ffn, ev_w_in, ev_g_v, ev_w_s, ev_b_s,
           ev_w_pool, ev_pool_scale, ev_w_out, od_w_in, od_lam_q1, od_lam_k1, od_lam_q2,
           od_lam_k2, od_g_sub, od_w_out, peer_w_q, peer_sub_keys, peer_u, peer_v, final_norm):
    B, S, D = x.shape
    depth = ada_w.shape[0]
    inv_freq = 1.0 / (ROPE_THETA ** (jnp.arange(0, DA_QK, 2, dtype=F32) / DA_QK))
    ang = positions.astype(F32)[..., None] * inv_freq
    cos_t = jnp.swapaxes(jnp.cos(ang), 1, 2)
    sin_t = jnp.swapaxes(jnp.sin(ang), 1, 2)
    mods = _ada(c, ada_w, ada_b)
    for l in range(depth):
        mod = mods[l]
        if l % 2 == 0:
            e = l // 2
            x = _even_layer(x, mod, norm_mix[l], ev_w_in[e], ev_g_v[e], ev_w_s[e], ev_b_s[e],
                            ev_w_pool[e], ev_pool_scale[e], ev_w_out[e])
        else:
            o = l // 2
            lam_init = 0.8 - 0.6 * math.exp(-0.3 * l)
            qt, k, vt = _qkv(x, mod, norm_mix[l], od_w_in[o], cos_t, sin_t)
            lam_vecs = jnp.stack([od_lam_q1[o], od_lam_k1[o], od_lam_q2[o], od_lam_k2[o]])
            ot = _attention(qt, k, vt, lam_vecs, od_g_sub[o], lam_init)
            x = _oproj(ot, od_w_out[o], x, mod)
        pt = _peer(x, mod, norm_ffn[l], peer_w_q[l], peer_sub_keys[l],
                   peer_u[l].astype(BF16), peer_v[l].T.astype(BF16))
        g2 = mod[:, 5][:, None, :]
        x = x + g2 * pt.T.reshape(B, S, D)
    return _final_norm(x, final_norm)
```

```python
import functools
import math

import jax
import jax.numpy as jnp
from jax import lax
from jax.experimental import pallas as pl
from jax.experimental.pallas import tpu as pltpu

F32 = jnp.float32
BF16 = jnp.bfloat16
EPS = 1e-6
NEG_INF = float("-inf")

GM_GROUPS = 4
GM_CH = 128
GM_WIDTH = GM_GROUPS * GM_CH
GM_CHUNK = 128
POOL_WINDOWS = (2, 4, 8, 16)
POOL_CH = 128
POOL_WIDTH = len(POOL_WINDOWS) * POOL_CH
POOL_HALO = max(POOL_WINDOWS)
DA_HEADS = 8
DA_QK = 64
DA_V = 128
ROPE_THETA = 10000.0
PEER_HEADS = 8
PEER_NKEYS = 128
PEER_TOPK = 16

V7X_VMEM_LIMIT = 56 * 1024 * 1024

EVEN_TS = 512
QKV_TS = 512
ATT_TQ = 256
ATT_TK = 512
ROUTE_T = 256
DENSE_T = 512
DENSE_EC = 1024
DENSE_CHUNK = 2


def _params(sem):
    return pltpu.CompilerParams(dimension_semantics=sem, vmem_limit_bytes=V7X_VMEM_LIMIT)


def _gelu(x):
    return 0.5 * x * (1.0 + lax.erf(x * 0.7071067811865476))


def _norm_mod(x, gn, sc, sh):
    ms = jnp.mean(x * x, axis=-1, keepdims=True)
    return x * lax.rsqrt(ms + EPS) * gn * (1.0 + sc) + sh


def _ada_kernel(c_ref, w_ref, b_ref, o_ref):
    c = c_ref[...]
    ca = c * jax.nn.sigmoid(c)
    o_ref[0, 0] = jnp.dot(ca, w_ref[0], preferred_element_type=F32,
                          precision=lax.Precision.HIGHEST) + b_ref[0, 0]


def _ada(c, ada_w, ada_b):
    L, D, _ = ada_w.shape
    B = c.shape[0]
    out = pl.pallas_call(
        _ada_kernel,
        out_shape=jax.ShapeDtypeStruct((L, 6, B, D), F32),
        grid=(L, 6),
        in_specs=[
            pl.BlockSpec((B, D), lambda l, j: (0, 0)),
            pl.BlockSpec((1, D, D), lambda l, j: (l, 0, j)),
            pl.BlockSpec((1, 1, 1, D), lambda l, j: (l, j, 0, 0)),
        ],
        out_specs=pl.BlockSpec((1, 1, B, D), lambda l, j: (l, j, 0, 0)),
        compiler_params=_params(("arbitrary", "arbitrary")),
        name="ada",
    )(c, ada_w, ada_b.reshape(L, 6, 1, D))
    return jnp.swapaxes(out, 1, 2)


def _even_kernel(x_ref, mod_ref, gn_ref, win_ref, gv_ref, ws_ref, bs_ref, wp_ref, ls_ref,
                 wout_ref, o_ref, ext_ref, mix_ref, *, ts):
    si = pl.program_id(1)
    x = x_ref[0]
    sh1 = mod_ref[0, 0:1, :]
    sc1 = mod_ref[0, 1:2, :]
    g1 = mod_ref[0, 2:3, :]
    h = _norm_mod(x, gn_ref[...], sc1, sh1)
    proj = jnp.dot(h.astype(BF16), win_ref[...], preferred_element_type=F32)

    z = _gelu(proj[:, :2 * GM_WIDTH])
    u = z[:, :GM_WIDTH]
    v = z[:, GM_WIDTH:]
    mu = jnp.mean(v, axis=-1, keepdims=True)
    d = v - mu
    var = jnp.mean(d * d, axis=-1, keepdims=True)
    vn = (d * lax.rsqrt(var + EPS) * gv_ref[...]).astype(BF16)
    r = lax.broadcasted_iota(jnp.int32, (GM_CHUNK, GM_CHUNK), 0)
    cidx = lax.broadcasted_iota(jnp.int32, (GM_CHUNK, GM_CHUNK), 1)
    causal = r >= cidx
    for g in range(GM_GROUPS):
        wg = jnp.where(causal, ws_ref[g], 0.0).astype(BF16)
        for c in range(ts // GM_CHUNK):
            rs = slice(c * GM_CHUNK, (c + 1) * GM_CHUNK)
            cs = slice(g * GM_CH, (g + 1) * GM_CH)
            sv = jnp.dot(wg, vn[rs, cs], preferred_element_type=F32) + bs_ref[g]
            mix_ref[rs, cs] = (u[rs, cs] * sv).astype(BF16)

    p = proj[:, 2 * GM_WIDTH:]

    @pl.when(si == 0)
    def _():
        ext_ref[0:POOL_HALO, :] = jnp.zeros((POOL_HALO, POOL_WIDTH), F32)

    ext_ref[POOL_HALO:POOL_HALO + ts, :] = p
    t = si * ts + lax.broadcasted_iota(jnp.int32, (ts, POOL_CH), 0)
    for g, w in enumerate(POOL_WINDOWS):
        cs = slice(g * POOL_CH, (g + 1) * POOL_CH)
        acc = ext_ref[POOL_HALO:POOL_HALO + ts, cs]
        for k in range(1, w):
            acc = acc + ext_ref[POOL_HALO - k:POOL_HALO - k + ts, cs]
        cnt = jnp.minimum(t + 1, w).astype(F32)
        pooled = acc / cnt - p[:, cs]
        yb = jnp.dot(pooled.astype(BF16), wp_ref[g], preferred_element_type=F32) * ls_ref[:, cs]
        mix_ref[:, GM_WIDTH + g * POOL_CH:GM_WIDTH + (g + 1) * POOL_CH] = yb.astype(BF16)
    ext_ref[0:POOL_HALO, :] = ext_ref[ts:ts + POOL_HALO, :]

    y = jnp.dot(mix_ref[...], wout_ref[...], preferred_element_type=F32)
    o_ref[0] = x + g1 * y


def _even_layer(x, mod, gn, w_in, g_v, w_s, b_s, w_pool, ls, w_out):
    B, S, D = x.shape
    ts = EVEN_TS
    n_in = w_in.shape[1]
    bsb = jnp.broadcast_to(b_s[:, :, None], (GM_GROUPS, GM_CHUNK, GM_CH))
    return pl.pallas_call(
        functools.partial(_even_kernel, ts=ts),
        out_shape=jax.ShapeDtypeStruct((B, S, D), F32),
        grid=(B, S // ts),
        in_specs=[
            pl.BlockSpec((1, ts, D), lambda b, s: (b, s, 0)),
            pl.BlockSpec((1, 6, D), lambda b, s: (b, 0, 0)),
            pl.BlockSpec((1, D), lambda b, s: (0, 0)),
            pl.BlockSpec((D, n_in), lambda b, s: (0, 0)),
            pl.BlockSpec((1, GM_WIDTH), lambda b, s: (0, 0)),
            pl.BlockSpec((GM_GROUPS, GM_CHUNK, GM_CHUNK), lambda b, s: (0, 0, 0)),
            pl.BlockSpec((GM_GROUPS, GM_CHUNK, GM_CH), lambda b, s: (0, 0, 0)),
            pl.BlockSpec((len(POOL_WINDOWS), POOL_CH, POOL_CH), lambda b, s: (0, 0, 0)),
            pl.BlockSpec((1, POOL_WIDTH), lambda b, s: (0, 0)),
            pl.BlockSpec((GM_WIDTH + POOL_WIDTH, D), lambda b, s: (0, 0)),
        ],
        out_specs=pl.BlockSpec((1, ts, D), lambda b, s: (b, s, 0)),
        scratch_shapes=[
            pltpu.VMEM((POOL_HALO + ts, POOL_WIDTH), F32),
            pltpu.VMEM((ts, GM_WIDTH + POOL_WIDTH), BF16),
        ],
        compiler_params=_params(("arbitrary", "arbitrary")),
        name="even_mixer",
    )(x, mod, gn.reshape(1, D), w_in.astype(BF16), g_v.reshape(1, -1), w_s, bsb,
      w_pool.astype(BF16), ls.reshape(1, -1), w_out.astype(BF16))


def _qkv_kernel(x_ref, mod_ref, gn_ref, wt_ref, cos_ref, sin_ref, qt_ref, k_ref, vt_ref,
                kt_scr, *, d_model):
    x = x_ref[0]
    sh1 = mod_ref[0, 0:1, :]
    sc1 = mod_ref[0, 1:2, :]
    h = _norm_mod(x, gn_ref[...], sc1, sh1)
    ht = h.T.astype(BF16)
    pt = jnp.dot(wt_ref[...], ht, preferred_element_type=F32)
    cos = cos_ref[0]
    sin = sin_ref[0]
    half = DA_QK // 2
    scale = DA_QK ** -0.5 * math.log2(math.e)
    for g in range(2 * DA_HEADS):
        r0 = g * DA_QK
        t1 = pt[r0:r0 + half]
        t2 = pt[r0 + half:r0 + DA_QK]
        qt_ref[0, r0:r0 + half, :] = ((t1 * cos - t2 * sin) * scale).astype(BF16)
        qt_ref[0, r0 + half:r0 + DA_QK, :] = ((t2 * cos + t1 * sin) * scale).astype(BF16)
        k0 = d_model + r0
        t1 = pt[k0:k0 + half]
        t2 = pt[k0 + half:k0 + DA_QK]
        kt_scr[r0:r0 + half, :] = t1 * cos - t2 * sin
        kt_scr[r0 + half:r0 + DA_QK, :] = t2 * cos + t1 * sin
    k_ref[0] = kt_scr[...].T.astype(BF16)
    vt_ref[0] = pt[2 * d_model:3 * d_model].astype(BF16)


def _qkv(x, mod, gn, w_in, cos_t, sin_t):
    B, S, D = x.shape
    ts = QKV_TS
    wt = w_in.T.astype(BF16)
    return pl.pallas_call(
        functools.partial(_qkv_kernel, d_model=D),
        out_shape=(
            jax.ShapeDtypeStruct((B, D, S), BF16),
            jax.ShapeDtypeStruct((B, S, D), BF16),
            jax.ShapeDtypeStruct((B, D, S), BF16),
        ),
        grid=(B, S // ts),
        in_specs=[
            pl.BlockSpec((1, ts, D), lambda b, s: (b, s, 0)),
            pl.BlockSpec((1, 6, D), lambda b, s: (b, 0, 0)),
            pl.BlockSpec((1, D), lambda b, s: (0, 0)),
            pl.BlockSpec((3 * D, D), lambda b, s: (0, 0)),
            pl.BlockSpec((1, DA_QK // 2, ts), lambda b, s: (b, 0, s)),
            pl.BlockSpec((1, DA_QK // 2, ts), lambda b, s: (b, 0, s)),
        ],
        out_specs=(
            pl.BlockSpec((1, D, ts), lambda b, s: (b, 0, s)),
            pl.BlockSpec((1, ts, D), lambda b, s: (b, s, 0)),
            pl.BlockSpec((1, D, ts), lambda b, s: (b, 0, s)),
        ),
        scratch_shapes=[pltpu.VMEM((D, ts), F32)],
        compiler_params=_params(("arbitrary", "arbitrary")),
        name="qkv_rope",
    )(x, mod, gn.reshape(1, D), wt, cos_t, sin_t)


def _attn_kernel(qt_ref, k_ref, vt_ref, lam_ref, gs_ref, o_ref, m_scr, l_scr, acc_scr,
                 *, tq, tk, lam_init):
    qi = pl.program_id(2)
    qt = qt_ref[0]
    row = lax.broadcasted_iota(jnp.int32, qt.shape, 0)
    zero = jnp.zeros_like(qt)
    q12 = jnp.concatenate([jnp.where(row < DA_QK, qt, zero), jnp.where(row >= DA_QK, qt, zero)],
                          axis=1)

    m_scr[...] = jnp.full(m_scr.shape, NEG_INF, F32)
    l_scr[...] = jnp.zeros(l_scr.shape, F32)
    acc_scr[...] = jnp.zeros(acc_scr.shape, F32)

    def update(j, masked):
        off = pl.multiple_of(j * tk, tk)
        kb = k_ref[0, pl.ds(off, tk), :]
        vb = vt_ref[0, :, pl.ds(off, tk)]
        s = jnp.dot(kb, q12, preferred_element_type=F32)
        if masked:
            kpos = off + lax.broadcasted_iota(jnp.int32, s.shape, 0)
            lane = lax.broadcasted_iota(jnp.int32, s.shape, 1)
            qpos = qi * tq + jnp.where(lane >= tq, lane - tq, lane)
            s = jnp.where(kpos <= qpos, s, NEG_INF)
        m_old = m_scr[...]
        m_new = jnp.maximum(m_old, jnp.max(s, axis=0, keepdims=True))
        alpha = jnp.exp2(m_old - m_new)
        p = jnp.exp2(s - m_new)
        l_scr[...] = alpha * l_scr[...] + jnp.sum(p, axis=0, keepdims=True)
        acc_scr[...] = alpha * acc_scr[...] + jnp.dot(vb, p.astype(BF16),
                                                      preferred_element_type=F32)
        m_scr[...] = m_new

    n_full = (qi * tq) // tk

    def body(jj, carry):
        update(2 * jj, False)
        update(2 * jj + 1, False)
        return carry

    lax.fori_loop(0, n_full // 2, body, 0)

    @pl.when(n_full % 2 == 1)
    def _():
        update(n_full - 1, False)

    update(n_full, True)

    lv = lam_ref[...]
    la = jnp.sum(lv[0:1] * lv[1:2], axis=-1, keepdims=True)
    lb = jnp.sum(lv[2:3] * lv[3:4], axis=-1, keepdims=True)
    lam = jnp.exp(la) - jnp.exp(lb) + lam_init
    acc = acc_scr[...]
    l = l_scr[...]
    o = acc[:, :tq] / l[:, :tq] - lam * (acc[:, tq:] / l[:, tq:])
    ms = jnp.mean(o * o, axis=0, keepdims=True)
    o = o * lax.rsqrt(ms + EPS) * gs_ref[...] * (1.0 - lam_init)
    o_ref[0] = o.astype(BF16)


def _attention(qt, k, vt, lam_vecs, g_sub, lam_init):
    B, D, S = qt.shape
    tq, tk = ATT_TQ, ATT_TK
    assert tk % tq == 0 and S % tk == 0
    hd = 2 * DA_QK
    return pl.pallas_call(
        functools.partial(_attn_kernel, tq=tq, tk=tk, lam_init=lam_init),
        out_shape=jax.ShapeDtypeStruct((B, D, S), BF16),
        grid=(B, DA_HEADS, S // tq),
        in_specs=[
            pl.BlockSpec((1, hd, tq), lambda b, h, q: (b, h, q)),
            pl.BlockSpec((1, S, hd), lambda b, h, q: (b, 0, h)),
            pl.BlockSpec((1, DA_V, S), lambda b, h, q: (b, h, 0)),
            pl.BlockSpec((4, DA_QK), lambda b, h, q: (0, 0)),
            pl.BlockSpec((DA_V, 1), lambda b, h, q: (0, 0)),
        ],
        out_specs=pl.BlockSpec((1, DA_V, tq), lambda b, h, q: (b, h, q)),
        scratch_shapes=[
            pltpu.VMEM((1, 2 * tq), F32),
            pltpu.VMEM((1, 2 * tq), F32),
            pltpu.VMEM((DA_V, 2 * tq), F32),
        ],
        compiler_params=_params(("arbitrary", "arbitrary", "arbitrary")),
        name="diff_attention",
    )(qt, k, vt, lam_vecs, g_sub.reshape(DA_V, 1))


def _oproj_kernel(ot_ref, wt_ref, x_ref, mod_ref, o_ref):
    yt = jnp.dot(wt_ref[...], ot_ref[0], preferred_element_type=F32)
    g1 = mod_ref[0, 2:3, :]
    o_ref[0] = x_ref[0] + g1 * yt.T


def _oproj(ot, w_out, x, mod):
    B, S, D = x.shape
    ts = QKV_TS
    return pl.pallas_call(
        _oproj_kernel,
        out_shape=jax.ShapeDtypeStruct((B, S, D), F32),
        grid=(B, S // ts),
        in_specs=[
            pl.BlockSpec((1, D, ts), lambda b, s: (b, 0, s)),
            pl.BlockSpec((D, D), lambda b, s: (0, 0)),
            pl.BlockSpec((1, ts, D), lambda b, s: (b, s, 0)),
            pl.BlockSpec((1, 6, D), lambda b, s: (b, 0, 0)),
        ],
        out_specs=pl.BlockSpec((1, ts, D), lambda b, s: (b, s, 0)),
        compiler_params=_params(("arbitrary", "arbitrary")),
        name="attn_out_proj",
    )(ot, w_out.T.astype(BF16), x, mod)


def _top_values(s, k, want_rank=False):
    vals = []
    rank = jnp.full(s.shape, float(k), F32) if want_rank else None
    for r in range(k):
        m = jnp.max(s, axis=0, keepdims=True)
        vals.append(m)
        hit = s == m
        if want_rank:
            rank = jnp.where(hit, float(r), rank)
        s = jnp.where(hit, NEG_INF, s)
    return vals, rank


def _route_kernel(x_ref, mod_ref, gn_ref, wq_ref, sk_ref, ht_ref, cnt_ref, e1_ref, r2_ref, e2_ref):
    x = x_ref[...]
    sh2 = mod_ref[0, 3:4, :]
    sc2 = mod_ref[0, 4:5, :]
    h = _norm_mod(x, gn_ref[...], sc2, sh2)
    ht_ref[...] = h.T.astype(BF16)
    q = jnp.dot(h.astype(BF16), wq_ref[...], preferred_element_type=F32)
    K = PEER_TOPK
    nk = PEER_NKEYS
    for hd in range(PEER_HEADS):
        rs = slice(hd * nk, (hd + 1) * nk)
        s12 = []
        for p in range(2):
            c0 = (hd * 2 + p) * nk
            qhp = q[:, c0:c0 + nk].astype(BF16)
            s12.append(lax.dot_general(sk_ref[hd * 2 + p], qhp, (((1,), (1,)), ((), ())),
                                       preferred_element_type=F32))
        s1, s2 = s12
        a, _ = _top_values(s1, K)
        b, r2 = _top_values(s2, K, want_rank=True)
        b_lo = jnp.concatenate(b[:8], axis=0)
        b_hi = jnp.concatenate(b[8:], axis=0)
        a_hi = jnp.concatenate(a[8:], axis=0)
        cand = [a[k] + b_lo for k in range(8)] + [a[0] + b_hi, a_hi + b[0]]
        c, _ = _top_values(jnp.concatenate(cand, axis=0), K)
        tau = c[K - 1]
        zsum = jnp.ones_like(c[0])
        for k in range(1, K):
            zsum = zsum + jnp.exp(c[k] - c[0])
        cnt = jnp.zeros(s1.shape, F32)
        for l in range(K):
            cnt = cnt + jnp.where((s1 + b[l]) >= tau, 1.0, 0.0)
        cnt_ref[rs, :] = cnt
        e1_ref[rs, :] = jnp.exp(s1 - a[0])
        r2_ref[rs, :] = r2.astype(BF16)
        e2_ref[rs, :] = (jnp.exp(s2 - b[0]) / zsum).astype(BF16)


def _dense_kernel(ht_ref, u_ref, vt_ref, cnt_ref, e1_ref, r2_ref, e2_ref, o_ref, wa_scr,
                  *, ec_size, chunk):
    ec = pl.program_id(1)
    nk = PEER_NKEYS
    per = ec_size // nk
    T = o_ref.shape[1]

    @pl.when(ec == 0)
    def _():
        o_ref[...] = jnp.zeros(o_ref.shape, F32)

    def gate_block(ii):
        w = jnp.zeros((nk, T), BF16)
        for hd in range(PEER_HEADS):
            rs = slice(hd * nk, (hd + 1) * nk)
            base = pl.multiple_of(hd * nk + ec * per, 8)
            cnt_row = cnt_ref[pl.ds(base, per), :][ii:ii + 1, :]
            e1_row = e1_ref[pl.ds(base, per), :][ii:ii + 1, :]
            cb = jnp.broadcast_to(cnt_row, (nk, T)).astype(BF16)
            eb = jnp.broadcast_to(e1_row, (nk, T)).astype(BF16)
            sel = r2_ref[rs, :] < cb
            w = w + jnp.where(sel, e2_ref[rs, :], jnp.zeros((), BF16)) * eb
        return w

    for c in range(per // chunk):
        rows = slice(c * chunk * nk, (c + 1) * chunk * nk)
        act = jnp.dot(u_ref[rows, :], ht_ref[...], preferred_element_type=F32)
        g = _gelu(act).astype(BF16)
        w = jnp.concatenate([gate_block(c * chunk + j) for j in range(chunk)], axis=0)
        wa_scr[rows, :] = w * g
    o_ref[...] += jnp.dot(vt_ref[...], wa_scr[...], preferred_element_type=F32)


def _peer(x, mod, gn, w_q, sub_keys, u_bf, vt_bf):
    B, S, D = x.shape
    N = B * S
    H, NK = PEER_HEADS, PEER_NKEYS
    E = u_bf.shape[0]
    tr = ROUTE_T
    x2 = x.reshape(N, D)
    wq = w_q.reshape(D, H * 2 * NK).astype(BF16)
    sk = sub_keys.reshape(H * 2, NK, NK).astype(BF16)
    ht, cnt, e1, r2, e2 = pl.pallas_call(
        _route_kernel,
        out_shape=(
            jax.ShapeDtypeStruct((D, N), BF16),
            jax.ShapeDtypeStruct((H * NK, N), F32),
            jax.ShapeDtypeStruct((H * NK, N), F32),
            jax.ShapeDtypeStruct((H * NK, N), BF16),
            jax.ShapeDtypeStruct((H * NK, N), BF16),
        ),
        grid=(N // tr,),
        in_specs=[
            pl.BlockSpec((tr, D), lambda i: (i, 0)),
            pl.BlockSpec((1, 6, D), lambda i: ((i * tr) // S, 0, 0)),
            pl.BlockSpec((1, D), lambda i: (0, 0)),
            pl.BlockSpec((D, H * 2 * NK), lambda i: (0, 0)),
            pl.BlockSpec((H * 2, NK, NK), lambda i: (0, 0, 0)),
        ],
        out_specs=(
            pl.BlockSpec((D, tr), lambda i: (0, i)),
            pl.BlockSpec((H * NK, tr), lambda i: (0, i)),
            pl.BlockSpec((H * NK, tr), lambda i: (0, i)),
            pl.BlockSpec((H * NK, tr), lambda i: (0, i)),
            pl.BlockSpec((H * NK, tr), lambda i: (0, i)),
        ),
        compiler_params=_params(("arbitrary",)),
        name="peer_route",
    )(x2, mod, gn.reshape(1, D), wq, sk)

    td, ec = DENSE_T, DENSE_EC
    assert (ec // NK) % 8 == 0 and (ec // NK) % DENSE_CHUNK == 0
    rank_spec = pl.BlockSpec((H * NK, td), lambda t, e: (0, t))
    return pl.pallas_call(
        functools.partial(_dense_kernel, ec_size=ec, chunk=DENSE_CHUNK),
        out_shape=jax.ShapeDtypeStruct((D, N), F32),
        grid=(N // td, E // ec),
        in_specs=[
            pl.BlockSpec((D, td), lambda t, e: (0, t)),
            pl.BlockSpec((ec, D), lambda t, e: (e, 0)),
            pl.BlockSpec((D, ec), lambda t, e: (0, e)),
            rank_spec, rank_spec, rank_spec, rank_spec,
        ],
        out_specs=pl.BlockSpec((D, td), lambda t, e: (0, t)),
        scratch_shapes=[pltpu.VMEM((ec, td), BF16)],
        compiler_params=_params(("arbitrary", "arbitrary")),
        name="peer_experts",
    )(ht, u_bf, vt_bf, cnt, e1, r2, e2)


def _final_kernel(x_ref, g_ref, o_ref):
    x = x_ref[...]
    ms = jnp.mean(x * x, axis=-1, keepdims=True)
    o_ref[...] = x * lax.rsqrt(ms + EPS) * g_ref[...]


def _final_norm(x, g):
    B, S, D = x.shape
    N = B * S
    t = 1024
    out = pl.pallas_call(
        _final_kernel,
        out_shape=jax.ShapeDtypeStruct((N, D), F32),
        grid=(N // t,),
        in_specs=[pl.BlockSpec((t, D), lambda i: (i, 0)), pl.BlockSpec((1, D), lambda i: (0, 0))],
        out_specs=pl.BlockSpec((t, D), lambda i: (i, 0)),
        compiler_params=_params(("arbitrary",)),
        name="final_norm",
    )(x.reshape(N, D), g.reshape(1, D))
    return out.reshape(B, S, D)


def kernel(x, c, positions, ada_w, ada_b, norm_mix, norm_ffn, ev_w_in, ev_g_v, ev_w_s, ev_b_s,
           ev_w_pool, ev_pool_scale, ev_w_out, od_w_in, od_lam_q1, od_lam_k1, od_lam_q2,
           od_lam_k2, od_g_sub, od_w_out, peer_w_q, peer_sub_keys, peer_u, peer_v, final_norm):
    B, S, D = x.shape
    depth = ada_w.shape[0]
    inv_freq = 1.0 / (ROPE_THETA ** (jnp.arange(0, DA_QK, 2, dtype=F32) / DA_QK))
    ang = positions.astype(F32)[..., None] * inv_freq
    cos_t = jnp.swapaxes(jnp.cos(ang), 1, 2)
    sin_t = jnp.swapaxes(jnp.sin(ang), 1, 2)
    mods = _ada(c, ada_w, ada_b)
    for l in range(depth):
        mod = mods[l]
        if l % 2 == 0:
            e = l // 2
            x = _even_layer(x, mod, norm_mix[l], ev_w_in[e], ev_g_v[e], ev_w_s[e], ev_b_s[e],
                            ev_w_pool[e], ev_pool_scale[e], ev_w_out[e])
        else:
            o = l // 2
            lam_init = 0.8 - 0.6 * math.exp(-0.3 * l)
            qt, k, vt = _qkv(x, mod, norm_mix[l], od_w_in[o], cos_t, sin_t)
            lam_vecs = jnp.stack([od_lam_q1[o], od_lam_k1[o], od_lam_q2[o], od_lam_k2[o]])
            ot = _attention(qt, k, vt, lam_vecs, od_g_sub[o], lam_init)
            x = _oproj(ot, od_w_out[o], x, mod)
        pt = _peer(x, mod, norm_ffn[l], peer_w_q[l], peer_sub_keys[l],
                   peer_u[l].astype(BF16), peer_v[l].T.astype(BF16))
        g2 = mod[:, 5][:, None, :]
        x = x + g2 * pt.T.reshape(B, S, D)
    return _final_norm(x, final_norm)
```

```python
import functools
import math

import jax
import jax.numpy as jnp
from jax import lax
from jax.experimental import pallas as pl
from jax.experimental.pallas import tpu as pltpu

F32 = jnp.float32
BF16 = jnp.bfloat16
EPS = 1e-6
NEG_INF = float("-inf")

GM_GROUPS = 4
GM_CH = 128
GM_WIDTH = GM_GROUPS * GM_CH
GM_CHUNK = 128
POOL_WINDOWS = (2, 4, 8, 16)
POOL_CH = 128
POOL_WIDTH = len(POOL_WINDOWS) * POOL_CH
POOL_HALO = max(POOL_WINDOWS)
DA_HEADS = 8
DA_QK = 64
DA_V = 128
ROPE_THETA = 10000.0
PEER_HEADS = 8
PEER_NKEYS = 128
PEER_TOPK = 16

V7X_VMEM_LIMIT = 56 * 1024 * 1024

EVEN_TS = 512
QKV_TS = 512
ATT_TQ = 512
ATT_TK = 512
ROUTE_T = 256
DENSE_T = 512
DENSE_EC = 1024
DENSE_CHUNK = 2


def _params(sem):
    return pltpu.CompilerParams(dimension_semantics=sem, vmem_limit_bytes=V7X_VMEM_LIMIT)


def _gelu(x):
    return 0.5 * x * (1.0 + lax.erf(x * 0.7071067811865476))


def _norm_mod(x, gn, sc, sh):
    ms = jnp.mean(x * x, axis=-1, keepdims=True)
    return x * lax.rsqrt(ms + EPS) * gn * (1.0 + sc) + sh


def _ada_kernel(c_ref, w_ref, b_ref, o_ref):
    c = c_ref[...]
    ca = c * jax.nn.sigmoid(c)
    o_ref[0, 0] = jnp.dot(ca, w_ref[0], preferred_element_type=F32,
                          precision=lax.Precision.HIGHEST) + b_ref[0, 0]


def _ada(c, ada_w, ada_b):
    L, D, _ = ada_w.shape
    B = c.shape[0]
    out = pl.pallas_call(
        _ada_kernel,
        out_shape=jax.ShapeDtypeStruct((L, 6, B, D), F32),
        grid=(L, 6),
        in_specs=[
            pl.BlockSpec((B, D), lambda l, j: (0, 0)),
            pl.BlockSpec((1, D, D), lambda l, j: (l, 0, j)),
            pl.BlockSpec((1, 1, 1, D), lambda l, j: (l, j, 0, 0)),
        ],
        out_specs=pl.BlockSpec((1, 1, B, D), lambda l, j: (l, j, 0, 0)),
        compiler_params=_params(("arbitrary", "arbitrary")),
        name="ada",
    )(c, ada_w, ada_b.reshape(L, 6, 1, D))
    return jnp.swapaxes(out, 1, 2)


def _even_kernel(x_ref, mod_ref, gn_ref, win_ref, gv_ref, ws_ref, bs_ref, wp_ref, ls_ref,
                 wout_ref, o_ref, ext_ref, mix_ref, *, ts):
    si = pl.program_id(1)
    x = x_ref[0]
    sh1 = mod_ref[0, 0:1, :]
    sc1 = mod_ref[0, 1:2, :]
    g1 = mod_ref[0, 2:3, :]
    h = _norm_mod(x, gn_ref[...], sc1, sh1)
    proj = jnp.dot(h.astype(BF16), win_ref[...], preferred_element_type=F32)

    z = _gelu(proj[:, :2 * GM_WIDTH])
    u = z[:, :GM_WIDTH]
    v = z[:, GM_WIDTH:]
    mu = jnp.mean(v, axis=-1, keepdims=True)
    d = v - mu
    var = jnp.mean(d * d, axis=-1, keepdims=True)
    vn = (d * lax.rsqrt(var + EPS) * gv_ref[...]).astype(BF16)
    r = lax.broadcasted_iota(jnp.int32, (GM_CHUNK, GM_CHUNK), 0)
    cidx = lax.broadcasted_iota(jnp.int32, (GM_CHUNK, GM_CHUNK), 1)
    causal = r >= cidx
    for g in range(GM_GROUPS):
        wg = jnp.where(causal, ws_ref[g], 0.0).astype(BF16)
        for c in range(ts // GM_CHUNK):
            rs = slice(c * GM_CHUNK, (c + 1) * GM_CHUNK)
            cs = slice(g * GM_CH, (g + 1) * GM_CH)
            sv = jnp.dot(wg, vn[rs, cs], preferred_element_type=F32) + bs_ref[g]
            mix_ref[rs, cs] = (u[rs, cs] * sv).astype(BF16)

    p = proj[:, 2 * GM_WIDTH:]

    @pl.when(si == 0)
    def _():
        ext_ref[0:POOL_HALO, :] = jnp.zeros((POOL_HALO, POOL_WIDTH), F32)

    ext_ref[POOL_HALO:POOL_HALO + ts, :] = p
    t = si * ts + lax.broadcasted_iota(jnp.int32, (ts, POOL_CH), 0)
    for g, w in enumerate(POOL_WINDOWS):
        cs = slice(g * POOL_CH, (g + 1) * POOL_CH)
        acc = ext_ref[POOL_HALO:POOL_HALO + ts, cs]
        for k in range(1, w):
            acc = acc + ext_ref[POOL_HALO - k:POOL_HALO - k + ts, cs]
        cnt = jnp.minimum(t + 1, w).astype(F32)
        pooled = acc / cnt - p[:, cs]
        yb = jnp.dot(pooled.astype(BF16), wp_ref[g], preferred_element_type=F32) * ls_ref[:, cs]
        mix_ref[:, GM_WIDTH + g * POOL_CH:GM_WIDTH + (g + 1) * POOL_CH] = yb.astype(BF16)
    ext_ref[0:POOL_HALO, :] = ext_ref[ts:ts + POOL_HALO, :]

    y = jnp.dot(mix_ref[...], wout_ref[...], preferred_element_type=F32)
    o_ref[0] = x + g1 * y


def _even_layer(x, mod, gn, w_in, g_v, w_s, b_s, w_pool, ls, w_out):
    B, S, D = x.shape
    ts = EVEN_TS
    n_in = w_in.shape[1]
    bsb = jnp.broadcast_to(b_s[:, :, None], (GM_GROUPS, GM_CHUNK, GM_CH))
    return pl.pallas_call(
        functools.partial(_even_kernel, ts=ts),
        out_shape=jax.ShapeDtypeStruct((B, S, D), F32),
        grid=(B, S // ts),
        in_specs=[
            pl.BlockSpec((1, ts, D), lambda b, s: (b, s, 0)),
            pl.BlockSpec((1, 6, D), lambda b, s: (b, 0, 0)),
            pl.BlockSpec((1, D), lambda b, s: (0, 0)),
            pl.BlockSpec((D, n_in), lambda b, s: (0, 0)),
            pl.BlockSpec((1, GM_WIDTH), lambda b, s: (0, 0)),
            pl.BlockSpec((GM_GROUPS, GM_CHUNK, GM_CHUNK), lambda b, s: (0, 0, 0)),
            pl.BlockSpec((GM_GROUPS, GM_CHUNK, GM_CH), lambda b, s: (0, 0, 0)),
            pl.BlockSpec((len(POOL_WINDOWS), POOL_CH, POOL_CH), lambda b, s: (0, 0, 0)),
            pl.BlockSpec((1, POOL_WIDTH), lambda b, s: (0, 0)),
            pl.BlockSpec((GM_WIDTH + POOL_WIDTH, D), lambda b, s: (0, 0)),
        ],
        out_specs=pl.BlockSpec((1, ts, D), lambda b, s: (b, s, 0)),
        scratch_shapes=[
            pltpu.VMEM((POOL_HALO + ts, POOL_WIDTH), F32),
            pltpu.VMEM((ts, GM_WIDTH + POOL_WIDTH), BF16),
        ],
        compiler_params=_params(("arbitrary", "arbitrary")),
        name="even_mixer",
    )(x, mod, gn.reshape(1, D), w_in.astype(BF16), g_v.reshape(1, -1), w_s, bsb,
      w_pool.astype(BF16), ls.reshape(1, -1), w_out.astype(BF16))


def _qkv_kernel(x_ref, mod_ref, gn_ref, wt_ref, cos_ref, sin_ref, qt_ref, k_ref, vt_ref,
                kt_scr, *, d_model):
    x = x_ref[0]
    sh1 = mod_ref[0, 0:1, :]
    sc1 = mod_ref[0, 1:2, :]
    h = _norm_mod(x, gn_ref[...], sc1, sh1)
    ht = h.T.astype(BF16)
    pt = jnp.dot(wt_ref[...], ht, preferred_element_type=F32)
    cos = cos_ref[0]
    sin = sin_ref[0]
    half = DA_QK // 2
    scale = DA_QK ** -0.5 * math.log2(math.e)
    for g in range(2 * DA_HEADS):
        r0 = g * DA_QK
        t1 = pt[r0:r0 + half]
        t2 = pt[r0 + half:r0 + DA_QK]
        qt_ref[0, r0:r0 + half, :] = ((t1 * cos - t2 * sin) * scale).astype(BF16)
        qt_ref[0, r0 + half:r0 + DA_QK, :] = ((t2 * cos + t1 * sin) * scale).astype(BF16)
        k0 = d_model + r0
        t1 = pt[k0:k0 + half]
        t2 = pt[k0 + half:k0 + DA_QK]
        kt_scr[r0:r0 + half, :] = t1 * cos - t2 * sin
        kt_scr[r0 + half:r0 + DA_QK, :] = t2 * cos + t1 * sin
    k_ref[0] = kt_scr[...].T.astype(BF16)
    vt_ref[0] = pt[2 * d_model:3 * d_model].astype(BF16)


def _qkv(x, mod, gn, w_in, cos_t, sin_t):
    B, S, D = x.shape
    ts = QKV_TS
    wt = w_in.T.astype(BF16)
    return pl.pallas_call(
        functools.partial(_qkv_kernel, d_model=D),
        out_shape=(
            jax.ShapeDtypeStruct((B, D, S), BF16),
            jax.ShapeDtypeStruct((B, S, D), BF16),
            jax.ShapeDtypeStruct((B, D, S), BF16),
        ),
        grid=(B, S // ts),
        in_specs=[
            pl.BlockSpec((1, ts, D), lambda b, s: (b, s, 0)),
            pl.BlockSpec((1, 6, D), lambda b, s: (b, 0, 0)),
            pl.BlockSpec((1, D), lambda b, s: (0, 0)),
            pl.BlockSpec((3 * D, D), lambda b, s: (0, 0)),
            pl.BlockSpec((1, DA_QK // 2, ts), lambda b, s: (b, 0, s)),
            pl.BlockSpec((1, DA_QK // 2, ts), lambda b, s: (b, 0, s)),
        ],
        out_specs=(
            pl.BlockSpec((1, D, ts), lambda b, s: (b, 0, s)),
            pl.BlockSpec((1, ts, D), lambda b, s: (b, s, 0)),
            pl.BlockSpec((1, D, ts), lambda b, s: (b, 0, s)),
        ),
        scratch_shapes=[pltpu.VMEM((D, ts), F32)],
        compiler_params=_params(("arbitrary", "arbitrary")),
        name="qkv_rope",
    )(x, mod, gn.reshape(1, D), wt, cos_t, sin_t)


def _attn_kernel(qt_ref, k_ref, vt_ref, lam_ref, gs_ref, o_ref, m_scr, l_scr, acc_scr,
                 *, tq, tk, lam_init):
    qi = pl.program_id(2)
    qt = qt_ref[0]
    row = lax.broadcasted_iota(jnp.int32, qt.shape, 0)
    zero = jnp.zeros_like(qt)
    q12 = jnp.concatenate([jnp.where(row < DA_QK, qt, zero), jnp.where(row >= DA_QK, qt, zero)],
                          axis=1)

    m_scr[...] = jnp.full(m_scr.shape, NEG_INF, F32)
    l_scr[...] = jnp.zeros(l_scr.shape, F32)
    acc_scr[...] = jnp.zeros(acc_scr.shape, F32)

    def update(j, masked):
        off = pl.multiple_of(j * tk, tk)
        kb = k_ref[0, pl.ds(off, tk), :]
        vb = vt_ref[0, :, pl.ds(off, tk)]
        s = jnp.dot(kb, q12, preferred_element_type=F32)
        if masked:
            kpos = off + lax.broadcasted_iota(jnp.int32, s.shape, 0)
            lane = lax.broadcasted_iota(jnp.int32, s.shape, 1)
            qpos = qi * tq + jnp.where(lane >= tq, lane - tq, lane)
            s = jnp.where(kpos <= qpos, s, NEG_INF)
        m_old = m_scr[...]
        m_new = jnp.maximum(m_old, jnp.max(s, axis=0, keepdims=True))
        alpha = jnp.exp2(m_old - m_new)
        p = jnp.exp2(s - m_new)
        l_scr[...] = alpha * l_scr[...] + jnp.sum(p, axis=0, keepdims=True)
        acc_scr[...] = alpha * acc_scr[...] + jnp.dot(vb, p.astype(BF16),
                                                      preferred_element_type=F32)
        m_scr[...] = m_new

    n_full = (qi * tq) // tk

    def body(jj, carry):
        update(2 * jj, False)
        update(2 * jj + 1, False)
        return carry

    lax.fori_loop(0, n_full // 2, body, 0)

    @pl.when(n_full % 2 == 1)
    def _():
        update(n_full - 1, False)

    update(n_full, True)

    lv = lam_ref[...]
    la = jnp.sum(lv[0:1] * lv[1:2], axis=-1, keepdims=True)
    lb = jnp.sum(lv[2:3] * lv[3:4], axis=-1, keepdims=True)
    lam = jnp.exp(la) - jnp.exp(lb) + lam_init
    acc = acc_scr[...]
    l = l_scr[...]
    o = acc[:, :tq] / l[:, :tq] - lam * (acc[:, tq:] / l[:, tq:])
    ms = jnp.mean(o * o, axis=0, keepdims=True)
    o = o * lax.rsqrt(ms + EPS) * gs_ref[...] * (1.0 - lam_init)
    o_ref[0] = o.astype(BF16)


def _attention(qt, k, vt, lam_vecs, g_sub, lam_init):
    B, D, S = qt.shape
    tq, tk = ATT_TQ, ATT_TK
    assert tk % tq == 0 and S % tk == 0
    hd = 2 * DA_QK
    return pl.pallas_call(
        functools.partial(_attn_kernel, tq=tq, tk=tk, lam_init=lam_init),
        out_shape=jax.ShapeDtypeStruct((B, D, S), BF16),
        grid=(B, DA_HEADS, S // tq),
        in_specs=[
            pl.BlockSpec((1, hd, tq), lambda b, h, q: (b, h, q)),
            pl.BlockSpec((1, S, hd), lambda b, h, q: (b, 0, h)),
            pl.BlockSpec((1, DA_V, S), lambda b, h, q: (b, h, 0)),
            pl.BlockSpec((4, DA_QK), lambda b, h, q: (0, 0)),
            pl.BlockSpec((DA_V, 1), lambda b, h, q: (0, 0)),
        ],
        out_specs=pl.BlockSpec((1, DA_V, tq), lambda b, h, q: (b, h, q)),
        scratch_shapes=[
            pltpu.VMEM((1, 2 * tq), F32),
            pltpu.VMEM((1, 2 * tq), F32),
            pltpu.VMEM((DA_V, 2 * tq), F32),
        ],
        compiler_params=_params(("arbitrary", "arbitrary", "arbitrary")),
        name="diff_attention",
    )(qt, k, vt, lam_vecs, g_sub.reshape(DA_V, 1))


def _oproj_kernel(ot_ref, wt_ref, x_ref, mod_ref, o_ref):
    yt = jnp.dot(wt_ref[...], ot_ref[0], preferred_element_type=F32)
    g1 = mod_ref[0, 2:3, :]
    o_ref[0] = x_ref[0] + g1 * yt.T


def _oproj(ot, w_out, x, mod):
    B, S, D = x.shape
    ts = QKV_TS
    return pl.pallas_call(
        _oproj_kernel,
        out_shape=jax.ShapeDtypeStruct((B, S, D), F32),
        grid=(B, S // ts),
        in_specs=[
            pl.BlockSpec((1, D, ts), lambda b, s: (b, 0, s)),
            pl.BlockSpec((D, D), lambda b, s: (0, 0)),
            pl.BlockSpec((1, ts, D), lambda b, s: (b, s, 0)),
            pl.BlockSpec((1, 6, D), lambda b, s: (b, 0, 0)),
        ],
        out_specs=pl.BlockSpec((1, ts, D), lambda b, s: (b, s, 0)),
        compiler_params=_params(("arbitrary", "arbitrary")),
        name="attn_out_proj",
    )(ot, w_out.T.astype(BF16), x, mod)


def _top_values(s, k, want_rank=False):
    vals = []
    rank = jnp.full(s.shape, float(k), F32) if want_rank else None
    for r in range(k):
        m = jnp.max(s, axis=0, keepdims=True)
        vals.append(m)
        hit = s == m
        if want_rank:
            rank = jnp.where(hit, float(r), rank)
        s = jnp.where(hit, NEG_INF, s)
    return vals, rank


def _route_kernel(x_ref, mod_ref, gn_ref, wq_ref, sk_ref, ht_ref, cnt_ref, e1_ref, r2_ref, e2_ref):
    x = x_ref[...]
    sh2 = mod_ref[0, 3:4, :]
    sc2 = mod_ref[0, 4:5, :]
    h = _norm_mod(x, gn_ref[...], sc2, sh2)
    ht_ref[...] = h.T.astype(BF16)
    q = jnp.dot(h.astype(BF16), wq_ref[...], preferred_element_type=F32)
    K = PEER_TOPK
    nk = PEER_NKEYS
    for hd in range(PEER_HEADS):
        rs = slice(hd * nk, (hd + 1) * nk)
        s12 = []
        for p in range(2):
            c0 = (hd * 2 + p) * nk
            qhp = q[:, c0:c0 + nk].astype(BF16)
            s12.append(lax.dot_general(sk_ref[hd * 2 + p], qhp, (((1,), (1,)), ((), ())),
                                       preferred_element_type=F32))
        s1, s2 = s12
        a, _ = _top_values(s1, K)
        b, r2 = _top_values(s2, K, want_rank=True)
        b_lo = jnp.concatenate(b[:8], axis=0)
        b_hi = jnp.concatenate(b[8:], axis=0)
        a_hi = jnp.concatenate(a[8:], axis=0)
        cand = [a[k] + b_lo for k in range(8)] + [a[0] + b_hi, a_hi + b[0]]
        c, _ = _top_values(jnp.concatenate(cand, axis=0), K)
        tau = c[K - 1]
        zsum = jnp.ones_like(c[0])
        for k in range(1, K):
            zsum = zsum + jnp.exp(c[k] - c[0])
        a_all = jnp.concatenate(a, axis=0)
        ck = jnp.zeros(a_all.shape, F32)
        for l in range(K):
            ck = ck + jnp.where((a_all + b[l]) >= tau, 1.0, 0.0)
        cnt = jnp.zeros(s1.shape, F32)
        for k in range(K):
            cnt = jnp.where(s1 == a[k], ck[k:k + 1, :], cnt)
        cnt_ref[rs, :] = cnt
        e1_ref[rs, :] = jnp.exp(s1 - a[0])
        r2_ref[rs, :] = r2.astype(BF16)
        e2_ref[rs, :] = (jnp.exp(s2 - b[0]) / zsum).astype(BF16)


def _dense_kernel(ht_ref, u_ref, v_ref, cnt_ref, e1_ref, r2_ref, e2_ref, x_ref, mod_ref, fn_ref,
                  o_ref, acc_scr, wa_scr, *, ec_size, chunk, n_steps, final_norm):
    ec = pl.program_id(1)
    nk = PEER_NKEYS
    per = ec_size // nk
    T = ht_ref.shape[1]

    @pl.when(ec == 0)
    def _():
        acc_scr[...] = jnp.zeros(acc_scr.shape, F32)

    def gate_block(ii):
        w = jnp.zeros((nk, T), BF16)
        for hd in range(PEER_HEADS):
            rs = slice(hd * nk, (hd + 1) * nk)
            base = pl.multiple_of(hd * nk + ec * per, 8)
            cnt_row = cnt_ref[pl.ds(base, per), :][ii:ii + 1, :]
            e1_row = e1_ref[pl.ds(base, per), :][ii:ii + 1, :]
            cb = jnp.broadcast_to(cnt_row, (nk, T)).astype(BF16)
            eb = jnp.broadcast_to(e1_row, (nk, T)).astype(BF16)
            sel = r2_ref[rs, :] < cb
            w = w + jnp.where(sel, e2_ref[rs, :], jnp.zeros((), BF16)) * eb
        return w

    for c in range(per // chunk):
        rows = slice(c * chunk * nk, (c + 1) * chunk * nk)
        act = jnp.dot(u_ref[rows, :], ht_ref[...], preferred_element_type=F32)
        g = _gelu(act).astype(BF16)
        w = jnp.concatenate([gate_block(c * chunk + j) for j in range(chunk)], axis=0)
        wa_scr[rows, :] = w * g
    acc_scr[...] += lax.dot_general(wa_scr[...], v_ref[...], (((0,), (0,)), ((), ())),
                                    preferred_element_type=F32)

    @pl.when(ec == n_steps - 1)
    def _():
        g2 = mod_ref[0, 5:6, :]
        y = x_ref[...] + g2 * acc_scr[...]
        if final_norm:
            ms = jnp.mean(y * y, axis=-1, keepdims=True)
            y = y * lax.rsqrt(ms + EPS) * fn_ref[...]
        o_ref[...] = y


def _peer(x, mod, gn, w_q, sub_keys, u_bf, v_bf, fn=None):
    B, S, D = x.shape
    N = B * S
    H, NK = PEER_HEADS, PEER_NKEYS
    E = u_bf.shape[0]
    tr = ROUTE_T
    x2 = x.reshape(N, D)
    wq = w_q.reshape(D, H * 2 * NK).astype(BF16)
    sk = sub_keys.reshape(H * 2, NK, NK).astype(BF16)
    ht, cnt, e1, r2, e2 = pl.pallas_call(
        _route_kernel,
        out_shape=(
            jax.ShapeDtypeStruct((D, N), BF16),
            jax.ShapeDtypeStruct((H * NK, N), F32),
            jax.ShapeDtypeStruct((H * NK, N), F32),
            jax.ShapeDtypeStruct((H * NK, N), BF16),
            jax.ShapeDtypeStruct((H * NK, N), BF16),
        ),
        grid=(N // tr,),
        in_specs=[
            pl.BlockSpec((tr, D), lambda i: (i, 0)),
            pl.BlockSpec((1, 6, D), lambda i: ((i * tr) // S, 0, 0)),
            pl.BlockSpec((1, D), lambda i: (0, 0)),
            pl.BlockSpec((D, H * 2 * NK), lambda i: (0, 0)),
            pl.BlockSpec((H * 2, NK, NK), lambda i: (0, 0, 0)),
        ],
        out_specs=(
            pl.BlockSpec((D, tr), lambda i: (0, i)),
            pl.BlockSpec((H * NK, tr), lambda i: (0, i)),
            pl.BlockSpec((H * NK, tr), lambda i: (0, i)),
            pl.BlockSpec((H * NK, tr), lambda i: (0, i)),
            pl.BlockSpec((H * NK, tr), lambda i: (0, i)),
        ),
        compiler_params=_params(("arbitrary",)),
        name="peer_route",
    )(x2, mod, gn.reshape(1, D), wq, sk)

    td, ec = DENSE_T, DENSE_EC
    assert (ec // NK) % 8 == 0 and (ec // NK) % DENSE_CHUNK == 0
    assert S % td == 0
    rank_spec = pl.BlockSpec((H * NK, td), lambda t, e: (0, t))
    fn_arr = jnp.ones((1, D), F32) if fn is None else fn.reshape(1, D)
    out = pl.pallas_call(
        functools.partial(_dense_kernel, ec_size=ec, chunk=DENSE_CHUNK, n_steps=E // ec,
                          final_norm=fn is not None),
        out_shape=jax.ShapeDtypeStruct((N, D), F32),
        grid=(N // td, E // ec),
        in_specs=[
            pl.BlockSpec((D, td), lambda t, e: (0, t)),
            pl.BlockSpec((ec, D), lambda t, e: (e, 0)),
            pl.BlockSpec((ec, D), lambda t, e: (e, 0)),
            rank_spec, rank_spec, rank_spec, rank_spec,
            pl.BlockSpec((td, D), lambda t, e: (t, 0)),
            pl.BlockSpec((1, 6, D), lambda t, e: ((t * td) // S, 0, 0)),
            pl.BlockSpec((1, D), lambda t, e: (0, 0)),
        ],
        out_specs=pl.BlockSpec((td, D), lambda t, e: (t, 0)),
        scratch_shapes=[pltpu.VMEM((td, D), F32), pltpu.VMEM((ec, td), BF16)],
        compiler_params=_params(("arbitrary", "arbitrary")),
        name="peer_experts",
    )(ht, u_bf, v_bf, cnt, e1, r2, e2, x2, mod, fn_arr)
    return out.reshape(B, S, D)


def kernel(x, c, positions, ada_w, ada_b, norm_mix, norm_ffn, ev_w_in, ev_g_v, ev_w_s, ev_b_s,
           ev_w_pool, ev_pool_scale, ev_w_out, od_w_in, od_lam_q1, od_lam_k1, od_lam_q2,
           od_lam_k2, od_g_sub, od_w_out, peer_w_q, peer_sub_keys, peer_u, peer_v, final_norm):
    B, S, D = x.shape
    depth = ada_w.shape[0]
    inv_freq = 1.0 / (ROPE_THETA ** (jnp.arange(0, DA_QK, 2, dtype=F32) / DA_QK))
    ang = positions.astype(F32)[..., None] * inv_freq
    cos_t = jnp.swapaxes(jnp.cos(ang), 1, 2)
    sin_t = jnp.swapaxes(jnp.sin(ang), 1, 2)
    mods = _ada(c, ada_w, ada_b)
    for l in range(depth):
        mod = mods[l]
        if l % 2 == 0:
            e = l // 2
            x = _even_layer(x, mod, norm_mix[l], ev_w_in[e], ev_g_v[e], ev_w_s[e], ev_b_s[e],
                            ev_w_pool[e], ev_pool_scale[e], ev_w_out[e])
        else:
            o = l // 2
            lam_init = 0.8 - 0.6 * math.exp(-0.3 * l)
            qt, k, vt = _qkv(x, mod, norm_mix[l], od_w_in[o], cos_t, sin_t)
            lam_vecs = jnp.stack([od_lam_q1[o], od_lam_k1[o], od_lam_q2[o], od_lam_k2[o]])
            ot = _attention(qt, k, vt, lam_vecs, od_g_sub[o], lam_init)
            x = _oproj(ot, od_w_out[o], x, mod)
        x = _peer(x, mod, norm_ffn[l], peer_w_q[l], peer_sub_keys[l],
                  peer_u[l].astype(BF16), peer_v[l].astype(BF16),
                  fn=final_norm if l == depth - 1 else None)
    return x
```

```python
import functools
import math

import jax
import jax.numpy as jnp
from jax import lax
from jax.experimental import pallas as pl
from jax.experimental.pallas import tpu as pltpu

F32 = jnp.float32
BF16 = jnp.bfloat16
EPS = 1e-6
NEG_INF = float("-inf")

GM_GROUPS = 4
GM_CH = 128
GM_WIDTH = GM_GROUPS * GM_CH
GM_CHUNK = 128
POOL_WINDOWS = (2, 4, 8, 16)
POOL_CH = 128
POOL_WIDTH = len(POOL_WINDOWS) * POOL_CH
POOL_HALO = max(POOL_WINDOWS)
DA_HEADS = 8
DA_QK = 64
DA_V = 128
ROPE_THETA = 10000.0
PEER_HEADS = 8
PEER_NKEYS = 128
PEER_TOPK = 16

V7X_VMEM_LIMIT = 56 * 1024 * 1024

EVEN_TS = 512
QKV_TS = 512
ATT_TQ = 512
ATT_TK = 512
ROUTE_T = 256
DENSE_T = 512
DENSE_EC = 1024
DENSE_CHUNK = 2


def _params(sem):
    return pltpu.CompilerParams(dimension_semantics=sem, vmem_limit_bytes=V7X_VMEM_LIMIT)


def _gelu(x):
    return 0.5 * x * (1.0 + lax.erf(x * 0.7071067811865476))


def _norm_mod(x, gn, sc, sh):
    ms = jnp.mean(x * x, axis=-1, keepdims=True)
    return x * lax.rsqrt(ms + EPS) * gn * (1.0 + sc) + sh


def _ada_kernel(c_ref, w_ref, b_ref, o_ref):
    c = c_ref[...]
    ca = c * jax.nn.sigmoid(c)
    o_ref[0, 0] = jnp.dot(ca, w_ref[0], preferred_element_type=F32,
                          precision=lax.Precision.HIGHEST) + b_ref[0, 0]


def _ada(c, ada_w, ada_b):
    L, D, _ = ada_w.shape
    B = c.shape[0]
    out = pl.pallas_call(
        _ada_kernel,
        out_shape=jax.ShapeDtypeStruct((L, 6, B, D), F32),
        grid=(L, 6),
        in_specs=[
            pl.BlockSpec((B, D), lambda l, j: (0, 0)),
            pl.BlockSpec((1, D, D), lambda l, j: (l, 0, j)),
            pl.BlockSpec((1, 1, 1, D), lambda l, j: (l, j, 0, 0)),
        ],
        out_specs=pl.BlockSpec((1, 1, B, D), lambda l, j: (l, j, 0, 0)),
        compiler_params=_params(("arbitrary", "arbitrary")),
        name="ada",
    )(c, ada_w, ada_b.reshape(L, 6, 1, D))
    return jnp.swapaxes(out, 1, 2)


def _even_kernel(x_ref, mod_ref, gn_ref, win_ref, gv_ref, ws_ref, bs_ref, wp_ref, ls_ref,
                 wout_ref, o_ref, ext_ref, mix_ref, *, ts):
    si = pl.program_id(1)
    x = x_ref[0]
    sh1 = mod_ref[0, 0:1, :]
    sc1 = mod_ref[0, 1:2, :]
    g1 = mod_ref[0, 2:3, :]
    h = _norm_mod(x, gn_ref[...], sc1, sh1)
    proj = jnp.dot(h.astype(BF16), win_ref[...], preferred_element_type=F32)

    z = _gelu(proj[:, :2 * GM_WIDTH])
    u = z[:, :GM_WIDTH]
    v = z[:, GM_WIDTH:]
    mu = jnp.mean(v, axis=-1, keepdims=True)
    d = v - mu
    var = jnp.mean(d * d, axis=-1, keepdims=True)
    vn = (d * lax.rsqrt(var + EPS) * gv_ref[...]).astype(BF16)
    r = lax.broadcasted_iota(jnp.int32, (GM_CHUNK, GM_CHUNK), 0)
    cidx = lax.broadcasted_iota(jnp.int32, (GM_CHUNK, GM_CHUNK), 1)
    causal = r >= cidx
    for g in range(GM_GROUPS):
        wg = jnp.where(causal, ws_ref[g], 0.0).astype(BF16)
        for c in range(ts // GM_CHUNK):
            rs = slice(c * GM_CHUNK, (c + 1) * GM_CHUNK)
            cs = slice(g * GM_CH, (g + 1) * GM_CH)
            sv = jnp.dot(wg, vn[rs, cs], preferred_element_type=F32) + bs_ref[g]
            mix_ref[rs, cs] = (u[rs, cs] * sv).astype(BF16)

    p = proj[:, 2 * GM_WIDTH:]

    @pl.when(si == 0)
    def _():
        ext_ref[0:POOL_HALO, :] = jnp.zeros((POOL_HALO, POOL_WIDTH), F32)

    ext_ref[POOL_HALO:POOL_HALO + ts, :] = p
    t = si * ts + lax.broadcasted_iota(jnp.int32, (ts, POOL_CH), 0)
    for g, w in enumerate(POOL_WINDOWS):
        cs = slice(g * POOL_CH, (g + 1) * POOL_CH)
        acc = ext_ref[POOL_HALO:POOL_HALO + ts, cs]
        for k in range(1, w):
            acc = acc + ext_ref[POOL_HALO - k:POOL_HALO - k + ts, cs]
        cnt = jnp.minimum(t + 1, w).astype(F32)
        pooled = acc / cnt - p[:, cs]
        yb = jnp.dot(pooled.astype(BF16), wp_ref[g], preferred_element_type=F32) * ls_ref[:, cs]
        mix_ref[:, GM_WIDTH + g * POOL_CH:GM_WIDTH + (g + 1) * POOL_CH] = yb.astype(BF16)
    ext_ref[0:POOL_HALO, :] = ext_ref[ts:ts + POOL_HALO, :]

    y = jnp.dot(mix_ref[...], wout_ref[...], preferred_element_type=F32)
    o_ref[0] = x + g1 * y


def _even_layer(x, mod, gn, w_in, g_v, w_s, b_s, w_pool, ls, w_out):
    B, S, D = x.shape
    ts = EVEN_TS
    n_in = w_in.shape[1]
    bsb = jnp.broadcast_to(b_s[:, :, None], (GM_GROUPS, GM_CHUNK, GM_CH))
    return pl.pallas_call(
        functools.partial(_even_kernel, ts=ts),
        out_shape=jax.ShapeDtypeStruct((B, S, D), F32),
        grid=(B, S // ts),
        in_specs=[
            pl.BlockSpec((1, ts, D), lambda b, s: (b, s, 0)),
            pl.BlockSpec((1, 6, D), lambda b, s: (b, 0, 0)),
            pl.BlockSpec((1, D), lambda b, s: (0, 0)),
            pl.BlockSpec((D, n_in), lambda b, s: (0, 0)),
            pl.BlockSpec((1, GM_WIDTH), lambda b, s: (0, 0)),
            pl.BlockSpec((GM_GROUPS, GM_CHUNK, GM_CHUNK), lambda b, s: (0, 0, 0)),
            pl.BlockSpec((GM_GROUPS, GM_CHUNK, GM_CH), lambda b, s: (0, 0, 0)),
            pl.BlockSpec((len(POOL_WINDOWS), POOL_CH, POOL_CH), lambda b, s: (0, 0, 0)),
            pl.BlockSpec((1, POOL_WIDTH), lambda b, s: (0, 0)),
            pl.BlockSpec((GM_WIDTH + POOL_WIDTH, D), lambda b, s: (0, 0)),
        ],
        out_specs=pl.BlockSpec((1, ts, D), lambda b, s: (b, s, 0)),
        scratch_shapes=[
            pltpu.VMEM((POOL_HALO + ts, POOL_WIDTH), F32),
            pltpu.VMEM((ts, GM_WIDTH + POOL_WIDTH), BF16),
        ],
        compiler_params=_params(("arbitrary", "arbitrary")),
        name="even_mixer",
    )(x, mod, gn.reshape(1, D), w_in.astype(BF16), g_v.reshape(1, -1), w_s, bsb,
      w_pool.astype(BF16), ls.reshape(1, -1), w_out.astype(BF16))


def _qkv_kernel(x_ref, mod_ref, gn_ref, wt_ref, cos_ref, sin_ref, qt_ref, k_ref, vt_ref,
                kt_scr, *, d_model):
    x = x_ref[0]
    sh1 = mod_ref[0, 0:1, :]
    sc1 = mod_ref[0, 1:2, :]
    h = _norm_mod(x, gn_ref[...], sc1, sh1)
    ht = h.T.astype(BF16)
    pt = jnp.dot(wt_ref[...], ht, preferred_element_type=F32)
    cos = cos_ref[0]
    sin = sin_ref[0]
    half = DA_QK // 2
    scale = DA_QK ** -0.5 * math.log2(math.e)
    for g in range(2 * DA_HEADS):
        r0 = g * DA_QK
        t1 = pt[r0:r0 + half]
        t2 = pt[r0 + half:r0 + DA_QK]
        qt_ref[0, r0:r0 + half, :] = ((t1 * cos - t2 * sin) * scale).astype(BF16)
        qt_ref[0, r0 + half:r0 + DA_QK, :] = ((t2 * cos + t1 * sin) * scale).astype(BF16)
        k0 = d_model + r0
        t1 = pt[k0:k0 + half]
        t2 = pt[k0 + half:k0 + DA_QK]
        kt_scr[r0:r0 + half, :] = t1 * cos - t2 * sin
        kt_scr[r0 + half:r0 + DA_QK, :] = t2 * cos + t1 * sin
    k_ref[0] = kt_scr[...].T.astype(BF16)
    vt_ref[0] = pt[2 * d_model:3 * d_model].astype(BF16)


def _qkv(x, mod, gn, w_in, cos_t, sin_t):
    B, S, D = x.shape
    ts = QKV_TS
    wt = w_in.T.astype(BF16)
    return pl.pallas_call(
        functools.partial(_qkv_kernel, d_model=D),
        out_shape=(
            jax.ShapeDtypeStruct((B, D, S), BF16),
            jax.ShapeDtypeStruct((B, S, D), BF16),
            jax.ShapeDtypeStruct((B, D, S), BF16),
        ),
        grid=(B, S // ts),
        in_specs=[
            pl.BlockSpec((1, ts, D), lambda b, s: (b, s, 0)),
            pl.BlockSpec((1, 6, D), lambda b, s: (b, 0, 0)),
            pl.BlockSpec((1, D), lambda b, s: (0, 0)),
            pl.BlockSpec((3 * D, D), lambda b, s: (0, 0)),
            pl.BlockSpec((1, DA_QK // 2, ts), lambda b, s: (b, 0, s)),
            pl.BlockSpec((1, DA_QK // 2, ts), lambda b, s: (b, 0, s)),
        ],
        out_specs=(
            pl.BlockSpec((1, D, ts), lambda b, s: (b, 0, s)),
            pl.BlockSpec((1, ts, D), lambda b, s: (b, s, 0)),
            pl.BlockSpec((1, D, ts), lambda b, s: (b, 0, s)),
        ),
        scratch_shapes=[pltpu.VMEM((D, ts), F32)],
        compiler_params=_params(("arbitrary", "arbitrary")),
        name="qkv_rope",
    )(x, mod, gn.reshape(1, D), wt, cos_t, sin_t)


def _attn_kernel(qt_ref, k_ref, vt_ref, lam_ref, gs_ref, o_ref, m_scr, l_scr, acc_scr,
                 *, tq, tk, lam_init):
    qi = pl.program_id(2)
    qt = qt_ref[0]
    row = lax.broadcasted_iota(jnp.int32, qt.shape, 0)
    zero = jnp.zeros_like(qt)
    q12 = jnp.concatenate([jnp.where(row < DA_QK, qt, zero), jnp.where(row >= DA_QK, qt, zero)],
                          axis=1)

    m_scr[...] = jnp.full(m_scr.shape, NEG_INF, F32)
    l_scr[...] = jnp.zeros(l_scr.shape, F32)
    acc_scr[...] = jnp.zeros(acc_scr.shape, F32)

    def update(j, masked):
        off = pl.multiple_of(j * tk, tk)
        kb = k_ref[0, pl.ds(off, tk), :]
        vb = vt_ref[0, :, pl.ds(off, tk)]
        s = jnp.dot(kb, q12, preferred_element_type=F32)
        if masked:
            kpos = off + lax.broadcasted_iota(jnp.int32, s.shape, 0)
            lane = lax.broadcasted_iota(jnp.int32, s.shape, 1)
            qpos = qi * tq + jnp.where(lane >= tq, lane - tq, lane)
            s = jnp.where(kpos <= qpos, s, NEG_INF)
        m_old = m_scr[...]
        m_new = jnp.maximum(m_old, jnp.max(s, axis=0, keepdims=True))
        alpha = jnp.exp2(m_old - m_new)
        p = jnp.exp2(s - m_new)
        l_scr[...] = alpha * l_scr[...] + jnp.sum(p, axis=0, keepdims=True)
        acc_scr[...] = alpha * acc_scr[...] + jnp.dot(vb, p.astype(BF16),
                                                      preferred_element_type=F32)
        m_scr[...] = m_new

    n_full = (qi * tq) // tk

    def body(jj, carry):
        update(2 * jj, False)
        update(2 * jj + 1, False)
        return carry

    lax.fori_loop(0, n_full // 2, body, 0)

    @pl.when(n_full % 2 == 1)
    def _():
        update(n_full - 1, False)

    update(n_full, True)

    lv = lam_ref[...]
    la = jnp.sum(lv[0:1] * lv[1:2], axis=-1, keepdims=True)
    lb = jnp.sum(lv[2:3] * lv[3:4], axis=-1, keepdims=True)
    lam = jnp.exp(la) - jnp.exp(lb) + lam_init
    acc = acc_scr[...]
    l = l_scr[...]
    o = acc[:, :tq] / l[:, :tq] - lam * (acc[:, tq:] / l[:, tq:])
    ms = jnp.mean(o * o, axis=0, keepdims=True)
    o = o * lax.rsqrt(ms + EPS) * gs_ref[...] * (1.0 - lam_init)
    o_ref[0] = o.astype(BF16)


def _attention(qt, k, vt, lam_vecs, g_sub, lam_init):
    B, D, S = qt.shape
    tq, tk = ATT_TQ, ATT_TK
    assert tk % tq == 0 and S % tk == 0
    hd = 2 * DA_QK
    return pl.pallas_call(
        functools.partial(_attn_kernel, tq=tq, tk=tk, lam_init=lam_init),
        out_shape=jax.ShapeDtypeStruct((B, D, S), BF16),
        grid=(B, DA_HEADS, S // tq),
        in_specs=[
            pl.BlockSpec((1, hd, tq), lambda b, h, q: (b, h, q)),
            pl.BlockSpec((1, S, hd), lambda b, h, q: (b, 0, h)),
            pl.BlockSpec((1, DA_V, S), lambda b, h, q: (b, h, 0)),
            pl.BlockSpec((4, DA_QK), lambda b, h, q: (0, 0)),
            pl.BlockSpec((DA_V, 1), lambda b, h, q: (0, 0)),
        ],
        out_specs=pl.BlockSpec((1, DA_V, tq), lambda b, h, q: (b, h, q)),
        scratch_shapes=[
            pltpu.VMEM((1, 2 * tq), F32),
            pltpu.VMEM((1, 2 * tq), F32),
            pltpu.VMEM((DA_V, 2 * tq), F32),
        ],
        compiler_params=_params(("arbitrary", "arbitrary", "arbitrary")),
        name="diff_attention",
    )(qt, k, vt, lam_vecs, g_sub.reshape(DA_V, 1))


def _oproj_kernel(ot_ref, wt_ref, x_ref, mod_ref, o_ref):
    yt = jnp.dot(wt_ref[...], ot_ref[0], preferred_element_type=F32)
    g1 = mod_ref[0, 2:3, :]
    o_ref[0] = x_ref[0] + g1 * yt.T


def _oproj(ot, w_out, x, mod):
    B, S, D = x.shape
    ts = QKV_TS
    return pl.pallas_call(
        _oproj_kernel,
        out_shape=jax.ShapeDtypeStruct((B, S, D), F32),
        grid=(B, S // ts),
        in_specs=[
            pl.BlockSpec((1, D, ts), lambda b, s: (b, 0, s)),
            pl.BlockSpec((D, D), lambda b, s: (0, 0)),
            pl.BlockSpec((1, ts, D), lambda b, s: (b, s, 0)),
            pl.BlockSpec((1, 6, D), lambda b, s: (b, 0, 0)),
        ],
        out_specs=pl.BlockSpec((1, ts, D), lambda b, s: (b, s, 0)),
        compiler_params=_params(("arbitrary", "arbitrary")),
        name="attn_out_proj",
    )(ot, w_out.T.astype(BF16), x, mod)


def _top_values(s, k, want_rank=False):
    vals = []
    rank = jnp.full(s.shape, float(k), F32) if want_rank else None
    for r in range(k):
        m = jnp.max(s, axis=0, keepdims=True)
        vals.append(m)
        hit = s == m
        if want_rank:
            rank = jnp.where(hit, float(r), rank)
        s = jnp.where(hit, NEG_INF, s)
    return vals, rank


def _route_kernel(x_ref, mod_ref, gn_ref, wq_ref, sk_ref, ht_ref, cnt_ref, e1_ref, r2_ref, e2_ref):
    x = x_ref[...]
    sh2 = mod_ref[0, 3:4, :]
    sc2 = mod_ref[0, 4:5, :]
    h = _norm_mod(x, gn_ref[...], sc2, sh2)
    ht_ref[...] = h.T.astype(BF16)
    q = jnp.dot(h.astype(BF16), wq_ref[...], preferred_element_type=F32)
    K = PEER_TOPK
    nk = PEER_NKEYS
    for hd in range(PEER_HEADS):
        rs = slice(hd * nk, (hd + 1) * nk)
        s12 = []
        for p in range(2):
            c0 = (hd * 2 + p) * nk
            qhp = q[:, c0:c0 + nk].astype(BF16)
            s12.append(lax.dot_general(sk_ref[hd * 2 + p], qhp, (((1,), (1,)), ((), ())),
                                       preferred_element_type=F32))
        s1, s2 = s12
        a, _ = _top_values(s1, K)
        b, r2 = _top_values(s2, K, want_rank=True)
        b_lo = jnp.concatenate(b[:8], axis=0)
        b_hi = jnp.concatenate(b[8:], axis=0)
        a_hi = jnp.concatenate(a[8:], axis=0)
        cand = [a[k] + b_lo for k in range(8)] + [a[0] + b_hi, a_hi + b[0]]
        c, _ = _top_values(jnp.concatenate(cand, axis=0), K)
        tau = c[K - 1]
        zsum = jnp.ones_like(c[0])
        for k in range(1, K):
            zsum = zsum + jnp.exp(c[k] - c[0])
        a_all = jnp.concatenate(a, axis=0)
        ck = jnp.zeros(a_all.shape, F32)
        for l in range(K):
            ck = ck + jnp.where((a_all + b[l]) >= tau, 1.0, 0.0)
        cnt = jnp.zeros(s1.shape, F32)
        for k in range(K):
            cnt = jnp.where(s1 == a[k], ck[k:k + 1, :], cnt)
        cnt_ref[rs, :] = cnt
        e1_ref[rs, :] = jnp.exp(s1 - a[0])
        r2_ref[rs, :] = r2.astype(BF16)
        e2_ref[rs, :] = (jnp.exp(s2 - b[0]) / zsum).astype(BF16)


def _dense_kernel(ht_ref, u_ref, vt_ref, cnt_ref, e1_ref, r2_ref, e2_ref, x_ref, mod_ref, fn_ref,
                  o_ref, acc_scr, wa_scr, *, ec_size, chunk, n_steps, final_norm):
    ec = pl.program_id(1)
    nk = PEER_NKEYS
    per = ec_size // nk
    T = ht_ref.shape[1]

    @pl.when(ec == 0)
    def _():
        acc_scr[...] = jnp.zeros(acc_scr.shape, F32)

    def gate_block(ii):
        w = jnp.zeros((nk, T), BF16)
        for hd in range(PEER_HEADS):
            rs = slice(hd * nk, (hd + 1) * nk)
            base = pl.multiple_of(hd * nk + ec * per, 8)
            cnt_row = cnt_ref[pl.ds(base, per), :][ii:ii + 1, :]
            e1_row = e1_ref[pl.ds(base, per), :][ii:ii + 1, :]
            cb = jnp.broadcast_to(cnt_row, (nk, T)).astype(BF16)
            eb = jnp.broadcast_to(e1_row, (nk, T)).astype(BF16)
            sel = r2_ref[rs, :] < cb
            w = w + jnp.where(sel, e2_ref[rs, :], jnp.zeros((), BF16)) * eb
        return w

    for c in range(per // chunk):
        rows = slice(c * chunk * nk, (c + 1) * chunk * nk)
        act = jnp.dot(u_ref[rows, :], ht_ref[...], preferred_element_type=F32)
        g = _gelu(act.astype(BF16))
        w = jnp.concatenate([gate_block(c * chunk + j) for j in range(chunk)], axis=0)
        wa_scr[rows, :] = w * g
    acc_scr[...] += jnp.dot(vt_ref[...], wa_scr[...], preferred_element_type=F32)

    @pl.when(ec == n_steps - 1)
    def _():
        g2 = mod_ref[0, 5:6, :]
        y = x_ref[...] + g2 * acc_scr[...].T
        if final_norm:
            ms = jnp.mean(y * y, axis=-1, keepdims=True)
            y = y * lax.rsqrt(ms + EPS) * fn_ref[...]
        o_ref[...] = y


def _peer(x, mod, gn, w_q, sub_keys, u_bf, vt_bf, fn=None):
    B, S, D = x.shape
    N = B * S
    H, NK = PEER_HEADS, PEER_NKEYS
    E = u_bf.shape[0]
    tr = ROUTE_T
    x2 = x.reshape(N, D)
    wq = w_q.reshape(D, H * 2 * NK).astype(BF16)
    sk = sub_keys.reshape(H * 2, NK, NK).astype(BF16)
    ht, cnt, e1, r2, e2 = pl.pallas_call(
        _route_kernel,
        out_shape=(
            jax.ShapeDtypeStruct((D, N), BF16),
            jax.ShapeDtypeStruct((H * NK, N), F32),
            jax.ShapeDtypeStruct((H * NK, N), F32),
            jax.ShapeDtypeStruct((H * NK, N), BF16),
            jax.ShapeDtypeStruct((H * NK, N), BF16),
        ),
        grid=(N // tr,),
        in_specs=[
            pl.BlockSpec((tr, D), lambda i: (i, 0)),
            pl.BlockSpec((1, 6, D), lambda i: ((i * tr) // S, 0, 0)),
            pl.BlockSpec((1, D), lambda i: (0, 0)),
            pl.BlockSpec((D, H * 2 * NK), lambda i: (0, 0)),
            pl.BlockSpec((H * 2, NK, NK), lambda i: (0, 0, 0)),
        ],
        out_specs=(
            pl.BlockSpec((D, tr), lambda i: (0, i)),
            pl.BlockSpec((H * NK, tr), lambda i: (0, i)),
            pl.BlockSpec((H * NK, tr), lambda i: (0, i)),
            pl.BlockSpec((H * NK, tr), lambda i: (0, i)),
            pl.BlockSpec((H * NK, tr), lambda i: (0, i)),
        ),
        compiler_params=_params(("arbitrary",)),
        name="peer_route",
    )(x2, mod, gn.reshape(1, D), wq, sk)

    td, ec = DENSE_T, DENSE_EC
    assert (ec // NK) % 8 == 0 and (ec // NK) % DENSE_CHUNK == 0
    assert S % td == 0
    rank_spec = pl.BlockSpec((H * NK, td), lambda t, e: (0, t))
    fn_arr = jnp.ones((1, D), F32) if fn is None else fn.reshape(1, D)
    out = pl.pallas_call(
        functools.partial(_dense_kernel, ec_size=ec, chunk=DENSE_CHUNK, n_steps=E // ec,
                          final_norm=fn is not None),
        out_shape=jax.ShapeDtypeStruct((N, D), F32),
        grid=(N // td, E // ec),
        in_specs=[
            pl.BlockSpec((D, td), lambda t, e: (0, t)),
            pl.BlockSpec((ec, D), lambda t, e: (e, 0)),
            pl.BlockSpec((D, ec), lambda t, e: (0, e)),
            rank_spec, rank_spec, rank_spec, rank_spec,
            pl.BlockSpec((td, D), lambda t, e: (t, 0)),
            pl.BlockSpec((1, 6, D), lambda t, e: ((t * td) // S, 0, 0)),
            pl.BlockSpec((1, D), lambda t, e: (0, 0)),
        ],
        out_specs=pl.BlockSpec((td, D), lambda t, e: (t, 0)),
        scratch_shapes=[pltpu.VMEM((D, td), F32), pltpu.VMEM((ec, td), BF16)],
        compiler_params=_params(("arbitrary", "arbitrary")),
        name="peer_experts",
    )(ht, u_bf, vt_bf, cnt, e1, r2, e2, x2, mod, fn_arr)
    return out.reshape(B, S, D)


def kernel(x, c, positions, ada_w, ada_b, norm_mix, norm_ffn, ev_w_in, ev_g_v, ev_w_s, ev_b_s,
           ev_w_pool, ev_pool_scale, ev_w_out, od_w_in, od_lam_q1, od_lam_k1, od_lam_q2,
           od_lam_k2, od_g_sub, od_w_out, peer_w_q, peer_sub_keys, peer_u, peer_v, final_norm):
    B, S, D = x.shape
    depth = ada_w.shape[0]
    inv_freq = 1.0 / (ROPE_THETA ** (jnp.arange(0, DA_QK, 2, dtype=F32) / DA_QK))
    ang = positions.astype(F32)[..., None] * inv_freq
    cos_t = jnp.swapaxes(jnp.cos(ang), 1, 2)
    sin_t = jnp.swapaxes(jnp.sin(ang), 1, 2)
    mods = _ada(c, ada_w, ada_b)
    for l in range(depth):
        mod = mods[l]
        if l % 2 == 0:
            e = l // 2
            x = _even_layer(x, mod, norm_mix[l], ev_w_in[e], ev_g_v[e], ev_w_s[e], ev_b_s[e],
                            ev_w_pool[e], ev_pool_scale[e], ev_w_out[e])
        else:
            o = l // 2
            lam_init = 0.8 - 0.6 * math.exp(-0.3 * l)
            qt, k, vt = _qkv(x, mod, norm_mix[l], od_w_in[o], cos_t, sin_t)
            lam_vecs = jnp.stack([od_lam_q1[o], od_lam_k1[o], od_lam_q2[o], od_lam_k2[o]])
            ot = _attention(qt, k, vt, lam_vecs, od_g_sub[o], lam_init)
            x = _oproj(ot, od_w_out[o], x, mod)
        x = _peer(x, mod, norm_ffn[l], peer_w_q[l], peer_sub_keys[l],
                  peer_u[l].astype(BF16), peer_v[l].T.astype(BF16),
                  fn=final_norm if l == depth - 1 else None)
    return x
```

```python
import functools
import math

import jax
import jax.numpy as jnp
from jax import lax
from jax.experimental import pallas as pl
from jax.experimental.pallas import tpu as pltpu

F32 = jnp.float32
BF16 = jnp.bfloat16
EPS = 1e-6
NEG_INF = float("-inf")

GM_GROUPS = 4
GM_CH = 128
GM_WIDTH = GM_GROUPS * GM_CH
GM_CHUNK = 128
POOL_WINDOWS = (2, 4, 8, 16)
POOL_CH = 128
POOL_WIDTH = len(POOL_WINDOWS) * POOL_CH
POOL_HALO = max(POOL_WINDOWS)
DA_HEADS = 8
DA_QK = 64
DA_V = 128
ROPE_THETA = 10000.0
PEER_HEADS = 8
PEER_NKEYS = 128
PEER_TOPK = 16

V7X_VMEM_LIMIT = 56 * 1024 * 1024

EVEN_TS = 512
QKV_TS = 512
ATT_TQ = 512
ATT_TK = 512
ROUTE_T = 256
DENSE_T = 512
DENSE_EC = 1024
DENSE_CHUNK = 2


def _params(sem):
    return pltpu.CompilerParams(dimension_semantics=sem, vmem_limit_bytes=V7X_VMEM_LIMIT)


def _gelu(x):
    return 0.5 * x * (1.0 + lax.erf(x * 0.7071067811865476))


def _norm_mod(x, gn, sc, sh):
    ms = jnp.mean(x * x, axis=-1, keepdims=True)
    return x * lax.rsqrt(ms + EPS) * gn * (1.0 + sc) + sh


def _ada_kernel(c_ref, w_ref, b_ref, o_ref):
    c = c_ref[...]
    ca = c * jax.nn.sigmoid(c)
    o_ref[0, 0] = jnp.dot(ca, w_ref[0], preferred_element_type=F32,
                          precision=lax.Precision.HIGHEST) + b_ref[0, 0]


def _ada(c, ada_w, ada_b):
    L, D, _ = ada_w.shape
    B = c.shape[0]
    out = pl.pallas_call(
        _ada_kernel,
        out_shape=jax.ShapeDtypeStruct((L, 6, B, D), F32),
        grid=(L, 6),
        in_specs=[
            pl.BlockSpec((B, D), lambda l, j: (0, 0)),
            pl.BlockSpec((1, D, D), lambda l, j: (l, 0, j)),
            pl.BlockSpec((1, 1, 1, D), lambda l, j: (l, j, 0, 0)),
        ],
        out_specs=pl.BlockSpec((1, 1, B, D), lambda l, j: (l, j, 0, 0)),
        compiler_params=_params(("arbitrary", "arbitrary")),
        name="ada",
    )(c, ada_w, ada_b.reshape(L, 6, 1, D))
    return jnp.swapaxes(out, 1, 2)


def _even_kernel(x_ref, mod_ref, gn_ref, win_ref, gv_ref, ws_ref, bs_ref, wp_ref, ls_ref,
                 wout_ref, o_ref, ext_ref, mix_ref, *, ts):
    si = pl.program_id(1)
    x = x_ref[0]
    sh1 = mod_ref[0, 0:1, :]
    sc1 = mod_ref[0, 1:2, :]
    g1 = mod_ref[0, 2:3, :]
    h = _norm_mod(x, gn_ref[...], sc1, sh1)
    proj = jnp.dot(h.astype(BF16), win_ref[...], preferred_element_type=F32)

    z = _gelu(proj[:, :2 * GM_WIDTH])
    u = z[:, :GM_WIDTH]
    v = z[:, GM_WIDTH:]
    mu = jnp.mean(v, axis=-1, keepdims=True)
    d = v - mu
    var = jnp.mean(d * d, axis=-1, keepdims=True)
    vn = (d * lax.rsqrt(var + EPS) * gv_ref[...]).astype(BF16)
    r = lax.broadcasted_iota(jnp.int32, (GM_CHUNK, GM_CHUNK), 0)
    cidx = lax.broadcasted_iota(jnp.int32, (GM_CHUNK, GM_CHUNK), 1)
    causal = r >= cidx
    for g in range(GM_GROUPS):
        wg = jnp.where(causal, ws_ref[g], 0.0).astype(BF16)
        for c in range(ts // GM_CHUNK):
            rs = slice(c * GM_CHUNK, (c + 1) * GM_CHUNK)
            cs = slice(g * GM_CH, (g + 1) * GM_CH)
            sv = jnp.dot(wg, vn[rs, cs], preferred_element_type=F32) + bs_ref[g]
            mix_ref[rs, cs] = (u[rs, cs] * sv).astype(BF16)

    p = proj[:, 2 * GM_WIDTH:]

    @pl.when(si == 0)
    def _():
        ext_ref[0:POOL_HALO, :] = jnp.zeros((POOL_HALO, POOL_WIDTH), F32)

    ext_ref[POOL_HALO:POOL_HALO + ts, :] = p
    t = si * ts + lax.broadcasted_iota(jnp.int32, (ts, POOL_CH), 0)
    for g, w in enumerate(POOL_WINDOWS):
        cs = slice(g * POOL_CH, (g + 1) * POOL_CH)
        acc = ext_ref[POOL_HALO:POOL_HALO + ts, cs]
        for k in range(1, w):
            acc = acc + ext_ref[POOL_HALO - k:POOL_HALO - k + ts, cs]
        cnt = jnp.minimum(t + 1, w).astype(F32)
        pooled = acc / cnt - p[:, cs]
        yb = jnp.dot(pooled.astype(BF16), wp_ref[g], preferred_element_type=F32) * ls_ref[:, cs]
        mix_ref[:, GM_WIDTH + g * POOL_CH:GM_WIDTH + (g + 1) * POOL_CH] = yb.astype(BF16)
    ext_ref[0:POOL_HALO, :] = ext_ref[ts:ts + POOL_HALO, :]

    y = jnp.dot(mix_ref[...], wout_ref[...], preferred_element_type=F32)
    o_ref[0] = x + g1 * y


def _even_layer(x, mod, gn, w_in, g_v, w_s, b_s, w_pool, ls, w_out):
    B, S, D = x.shape
    ts = EVEN_TS
    n_in = w_in.shape[1]
    bsb = jnp.broadcast_to(b_s[:, :, None], (GM_GROUPS, GM_CHUNK, GM_CH))
    return pl.pallas_call(
        functools.partial(_even_kernel, ts=ts),
        out_shape=jax.ShapeDtypeStruct((B, S, D), F32),
        grid=(B, S // ts),
        in_specs=[
            pl.BlockSpec((1, ts, D), lambda b, s: (b, s, 0)),
            pl.BlockSpec((1, 6, D), lambda b, s: (b, 0, 0)),
            pl.BlockSpec((1, D), lambda b, s: (0, 0)),
            pl.BlockSpec((D, n_in), lambda b, s: (0, 0)),
            pl.BlockSpec((1, GM_WIDTH), lambda b, s: (0, 0)),
            pl.BlockSpec((GM_GROUPS, GM_CHUNK, GM_CHUNK), lambda b, s: (0, 0, 0)),
            pl.BlockSpec((GM_GROUPS, GM_CHUNK, GM_CH), lambda b, s: (0, 0, 0)),
            pl.BlockSpec((len(POOL_WINDOWS), POOL_CH, POOL_CH), lambda b, s: (0, 0, 0)),
            pl.BlockSpec((1, POOL_WIDTH), lambda b, s: (0, 0)),
            pl.BlockSpec((GM_WIDTH + POOL_WIDTH, D), lambda b, s: (0, 0)),
        ],
        out_specs=pl.BlockSpec((1, ts, D), lambda b, s: (b, s, 0)),
        scratch_shapes=[
            pltpu.VMEM((POOL_HALO + ts, POOL_WIDTH), F32),
            pltpu.VMEM((ts, GM_WIDTH + POOL_WIDTH), BF16),
        ],
        compiler_params=_params(("arbitrary", "arbitrary")),
        name="even_mixer",
    )(x, mod, gn.reshape(1, D), w_in.astype(BF16), g_v.reshape(1, -1), w_s, bsb,
      w_pool.astype(BF16), ls.reshape(1, -1), w_out.astype(BF16))


def _qkv_kernel(x_ref, mod_ref, gn_ref, wt_ref, cos_ref, sin_ref, qt_ref, k_ref, vt_ref,
                kt_scr, *, d_model):
    x = x_ref[0]
    sh1 = mod_ref[0, 0:1, :]
    sc1 = mod_ref[0, 1:2, :]
    h = _norm_mod(x, gn_ref[...], sc1, sh1)
    ht = h.T.astype(BF16)
    pt = jnp.dot(wt_ref[...], ht, preferred_element_type=F32)
    cos = cos_ref[0]
    sin = sin_ref[0]
    half = DA_QK // 2
    scale = DA_QK ** -0.5 * math.log2(math.e)
    for g in range(2 * DA_HEADS):
        r0 = g * DA_QK
        t1 = pt[r0:r0 + half]
        t2 = pt[r0 + half:r0 + DA_QK]
        qt_ref[0, r0:r0 + half, :] = ((t1 * cos - t2 * sin) * scale).astype(BF16)
        qt_ref[0, r0 + half:r0 + DA_QK, :] = ((t2 * cos + t1 * sin) * scale).astype(BF16)
        k0 = d_model + r0
        t1 = pt[k0:k0 + half]
        t2 = pt[k0 + half:k0 + DA_QK]
        kt_scr[r0:r0 + half, :] = t1 * cos - t2 * sin
        kt_scr[r0 + half:r0 + DA_QK, :] = t2 * cos + t1 * sin
    k_ref[0] = kt_scr[...].T.astype(BF16)
    vt_ref[0] = pt[2 * d_model:3 * d_model].astype(BF16)


def _qkv(x, mod, gn, w_in, cos_t, sin_t):
    B, S, D = x.shape
    ts = QKV_TS
    wt = w_in.T.astype(BF16)
    return pl.pallas_call(
        functools.partial(_qkv_kernel, d_model=D),
        out_shape=(
            jax.ShapeDtypeStruct((B, D, S), BF16),
            jax.ShapeDtypeStruct((B, S, D), BF16),
            jax.ShapeDtypeStruct((B, D, S), BF16),
        ),
        grid=(B, S // ts),
        in_specs=[
            pl.BlockSpec((1, ts, D), lambda b, s: (b, s, 0)),
            pl.BlockSpec((1, 6, D), lambda b, s: (b, 0, 0)),
            pl.BlockSpec((1, D), lambda b, s: (0, 0)),
            pl.BlockSpec((3 * D, D), lambda b, s: (0, 0)),
            pl.BlockSpec((1, DA_QK // 2, ts), lambda b, s: (b, 0, s)),
            pl.BlockSpec((1, DA_QK // 2, ts), lambda b, s: (b, 0, s)),
        ],
        out_specs=(
            pl.BlockSpec((1, D, ts), lambda b, s: (b, 0, s)),
            pl.BlockSpec((1, ts, D), lambda b, s: (b, s, 0)),
            pl.BlockSpec((1, D, ts), lambda b, s: (b, 0, s)),
        ),
        scratch_shapes=[pltpu.VMEM((D, ts), F32)],
        compiler_params=_params(("arbitrary", "arbitrary")),
        name="qkv_rope",
    )(x, mod, gn.reshape(1, D), wt, cos_t, sin_t)


def _attn_kernel(qt_ref, k_ref, vt_ref, lam_ref, gs_ref, o_ref, *, tq, tk, lam_init):
    qi = pl.program_id(2)
    qt = qt_ref[0]
    row = lax.broadcasted_iota(jnp.int32, qt.shape, 0)
    zero = jnp.zeros_like(qt)
    q12 = jnp.concatenate([jnp.where(row < DA_QK, qt, zero), jnp.where(row >= DA_QK, qt, zero)],
                          axis=1)

    def update(j, masked, carry):
        m_old, l_old, acc_old = carry
        off = pl.multiple_of(j * tk, tk)
        kb = k_ref[0, pl.ds(off, tk), :]
        vb = vt_ref[0, :, pl.ds(off, tk)]
        s = jnp.dot(kb, q12, preferred_element_type=F32)
        if masked:
            kpos = off + lax.broadcasted_iota(jnp.int32, s.shape, 0)
            lane = lax.broadcasted_iota(jnp.int32, s.shape, 1)
            qpos = qi * tq + jnp.where(lane >= tq, lane - tq, lane)
            s = jnp.where(kpos <= qpos, s, NEG_INF)
        m_new = jnp.maximum(m_old, jnp.max(s, axis=0, keepdims=True))
        alpha = jnp.exp2(m_old - m_new)
        p = jnp.exp2(s - m_new)
        l_new = alpha * l_old + jnp.sum(p, axis=0, keepdims=True)
        acc_new = alpha * acc_old + jnp.dot(vb, p.astype(BF16), preferred_element_type=F32)
        return m_new, l_new, acc_new

    init = (jnp.full((1, 2 * tq), NEG_INF, F32), jnp.zeros((1, 2 * tq), F32),
            jnp.zeros((DA_V, 2 * tq), F32))
    n_full = (qi * tq) // tk

    def body(jj, carry):
        carry = update(2 * jj, False, carry)
        return update(2 * jj + 1, False, carry)

    carry = lax.fori_loop(0, n_full // 2, body, init)
    carry = lax.cond(n_full % 2 == 1, lambda c: update(n_full - 1, False, c), lambda c: c, carry)
    _, l, acc = update(n_full, True, carry)

    lv = lam_ref[...]
    la = jnp.sum(lv[0:1] * lv[1:2], axis=-1, keepdims=True)
    lb = jnp.sum(lv[2:3] * lv[3:4], axis=-1, keepdims=True)
    lam = jnp.exp(la) - jnp.exp(lb) + lam_init
    o = acc[:, :tq] / l[:, :tq] - lam * (acc[:, tq:] / l[:, tq:])
    ms = jnp.mean(o * o, axis=0, keepdims=True)
    o = o * lax.rsqrt(ms + EPS) * gs_ref[...] * (1.0 - lam_init)
    o_ref[0] = o.astype(BF16)


def _attention(qt, k, vt, lam_vecs, g_sub, lam_init):
    B, D, S = qt.shape
    tq, tk = ATT_TQ, ATT_TK
    assert tk % tq == 0 and S % tk == 0
    hd = 2 * DA_QK
    return pl.pallas_call(
        functools.partial(_attn_kernel, tq=tq, tk=tk, lam_init=lam_init),
        out_shape=jax.ShapeDtypeStruct((B, D, S), BF16),
        grid=(B, DA_HEADS, S // tq),
        in_specs=[
            pl.BlockSpec((1, hd, tq), lambda b, h, q: (b, h, q)),
            pl.BlockSpec((1, S, hd), lambda b, h, q: (b, 0, h)),
            pl.BlockSpec((1, DA_V, S), lambda b, h, q: (b, h, 0)),
            pl.BlockSpec((4, DA_QK), lambda b, h, q: (0, 0)),
            pl.BlockSpec((DA_V, 1), lambda b, h, q: (0, 0)),
        ],
        out_specs=pl.BlockSpec((1, DA_V, tq), lambda b, h, q: (b, h, q)),
        compiler_params=_params(("arbitrary", "arbitrary", "arbitrary")),
        name="diff_attention",
    )(qt, k, vt, lam_vecs, g_sub.reshape(DA_V, 1))


def _oproj_kernel(ot_ref, wt_ref, x_ref, mod_ref, o_ref):
    yt = jnp.dot(wt_ref[...], ot_ref[0], preferred_element_type=F32)
    g1 = mod_ref[0, 2:3, :]
    o_ref[0] = x_ref[0] + g1 * yt.T


def _oproj(ot, w_out, x, mod):
    B, S, D = x.shape
    ts = QKV_TS
    return pl.pallas_call(
        _oproj_kernel,
        out_shape=jax.ShapeDtypeStruct((B, S, D), F32),
        grid=(B, S // ts),
        in_specs=[
            pl.BlockSpec((1, D, ts), lambda b, s: (b, 0, s)),
            pl.BlockSpec((D, D), lambda b, s: (0, 0)),
            pl.BlockSpec((1, ts, D), lambda b, s: (b, s, 0)),
            pl.BlockSpec((1, 6, D), lambda b, s: (b, 0, 0)),
        ],
        out_specs=pl.BlockSpec((1, ts, D), lambda b, s: (b, s, 0)),
        compiler_params=_params(("arbitrary", "arbitrary")),
        name="attn_out_proj",
    )(ot, w_out.T.astype(BF16), x, mod)


def _top_values(s, k):
    vals = []
    for _ in range(k):
        m = jnp.max(s, axis=0, keepdims=True)
        vals.append(m)
        s = jnp.where(s == m, NEG_INF, s)
    return vals


def _top_values_128(s, k):
    n = s.shape[0] // 8
    assert n == 16 and k <= n
    blocks = [s[8 * v:8 * v + 8, :] for v in range(n)]
    size = 2
    while size <= n:
        stride = size // 2
        while stride >= 1:
            for i in range(n):
                j = i ^ stride
                if j > i:
                    hi = jnp.maximum(blocks[i], blocks[j])
                    lo = jnp.minimum(blocks[i], blocks[j])
                    blocks[i], blocks[j] = (hi, lo) if (i & size) == 0 else (lo, hi)
            stride //= 2
        size *= 2
    vals = []
    for t in range(k):
        head = blocks[0]
        m = jnp.max(head, axis=0, keepdims=True)
        vals.append(m)
        hit = head == m
        for r in range(k - 1 - t):
            blocks[r] = jnp.where(hit, blocks[r + 1], blocks[r])
    return vals


def _route_kernel(x_ref, mod_ref, gn_ref, wq_ref, sk_ref, ht_ref, cnt_ref, e1_ref, r2_ref, e2_ref):
    x = x_ref[...]
    sh2 = mod_ref[0, 3:4, :]
    sc2 = mod_ref[0, 4:5, :]
    h = _norm_mod(x, gn_ref[...], sc2, sh2)
    ht_ref[...] = h.T.astype(BF16)
    q = jnp.dot(h.astype(BF16), wq_ref[...], preferred_element_type=F32)
    K = PEER_TOPK
    nk = PEER_NKEYS
    for hd in range(PEER_HEADS):
        rs = slice(hd * nk, (hd + 1) * nk)
        s12 = []
        for p in range(2):
            c0 = (hd * 2 + p) * nk
            qhp = q[:, c0:c0 + nk].astype(BF16)
            s12.append(lax.dot_general(sk_ref[hd * 2 + p], qhp, (((1,), (1,)), ((), ())),
                                       preferred_element_type=F32))
        s1, s2 = s12
        a = _top_values_128(s1, K)
        b = _top_values_128(s2, K)
        r2 = jnp.full(s2.shape, float(K), F32)
        for l in reversed(range(K)):
            r2 = jnp.where(s2 >= b[l], float(l), r2)
        b_lo = jnp.concatenate(b[:8], axis=0)
        b_hi = jnp.concatenate(b[8:], axis=0)
        a_hi = jnp.concatenate(a[8:], axis=0)
        cand = [a[k] + b_lo for k in range(8)] + [a[0] + b_hi, a_hi + b[0]]
        c = _top_values(jnp.concatenate(cand, axis=0), K)
        tau = c[K - 1]
        zsum = jnp.ones_like(c[0])
        for k in range(1, K):
            zsum = zsum + jnp.exp(c[k] - c[0])
        a_all = jnp.concatenate(a, axis=0)
        ck = jnp.zeros(a_all.shape, F32)
        for l in range(K):
            ck = ck + jnp.where((a_all + b[l]) >= tau, 1.0, 0.0)
        cnt = jnp.zeros(s1.shape, F32)
        for k in range(K):
            cnt = jnp.where(s1 == a[k], ck[k:k + 1, :], cnt)
        cnt_ref[rs, :] = cnt
        e1_ref[rs, :] = jnp.exp(s1 - a[0])
        r2_ref[rs, :] = r2.astype(BF16)
        e2_ref[rs, :] = (jnp.exp(s2 - b[0]) / zsum).astype(BF16)


def _dense_kernel(ht_ref, u_ref, vt_ref, cnt_ref, e1_ref, r2_ref, e2_ref, x_ref, mod_ref, fn_ref,
                  o_ref, acc_scr, wa_scr, *, ec_size, chunk, n_steps, final_norm):
    ec = pl.program_id(1)
    nk = PEER_NKEYS
    per = ec_size // nk
    T = ht_ref.shape[1]

    @pl.when(ec == 0)
    def _():
        acc_scr[...] = jnp.zeros(acc_scr.shape, F32)

    def gate_block(ii):
        w = jnp.zeros((nk, T), BF16)
        for hd in range(PEER_HEADS):
            rs = slice(hd * nk, (hd + 1) * nk)
            base = pl.multiple_of(hd * nk + ec * per, 8)
            cnt_row = cnt_ref[pl.ds(base, per), :][ii:ii + 1, :]
            e1_row = e1_ref[pl.ds(base, per), :][ii:ii + 1, :]
            cb = jnp.broadcast_to(cnt_row, (nk, T)).astype(BF16)
            eb = jnp.broadcast_to(e1_row, (nk, T)).astype(BF16)
            sel = r2_ref[rs, :] < cb
            w = w + jnp.where(sel, e2_ref[rs, :], jnp.zeros((), BF16)) * eb
        return w

    for c in range(per // chunk):
        rows = slice(c * chunk * nk, (c + 1) * chunk * nk)
        act = jnp.dot(u_ref[rows, :], ht_ref[...], preferred_element_type=F32)
        g = _gelu(act.astype(BF16))
        w = jnp.concatenate([gate_block(c * chunk + j) for j in range(chunk)], axis=0)
        wa_scr[rows, :] = w * g
    acc_scr[...] += jnp.dot(vt_ref[...], wa_scr[...], preferred_element_type=F32)

    @pl.when(ec == n_steps - 1)
    def _():
        g2 = mod_ref[0, 5:6, :]
        y = x_ref[...] + g2 * acc_scr[...].T
        if final_norm:
            ms = jnp.mean(y * y, axis=-1, keepdims=True)
            y = y * lax.rsqrt(ms + EPS) * fn_ref[...]
        o_ref[...] = y


def _peer(x, mod, gn, w_q, sub_keys, u_bf, vt_bf, fn=None):
    B, S, D = x.shape
    N = B * S
    H, NK = PEER_HEADS, PEER_NKEYS
    E = u_bf.shape[0]
    tr = ROUTE_T
    x2 = x.reshape(N, D)
    wq = w_q.reshape(D, H * 2 * NK).astype(BF16)
    sk = sub_keys.reshape(H * 2, NK, NK).astype(BF16)
    ht, cnt, e1, r2, e2 = pl.pallas_call(
        _route_kernel,
        out_shape=(
            jax.ShapeDtypeStruct((D, N), BF16),
            jax.ShapeDtypeStruct((H * NK, N), F32),
            jax.ShapeDtypeStruct((H * NK, N), F32),
            jax.ShapeDtypeStruct((H * NK, N), BF16),
            jax.ShapeDtypeStruct((H * NK, N), BF16),
        ),
        grid=(N // tr,),
        in_specs=[
            pl.BlockSpec((tr, D), lambda i: (i, 0)),
            pl.BlockSpec((1, 6, D), lambda i: ((i * tr) // S, 0, 0)),
            pl.BlockSpec((1, D), lambda i: (0, 0)),
            pl.BlockSpec((D, H * 2 * NK), lambda i: (0, 0)),
            pl.BlockSpec((H * 2, NK, NK), lambda i: (0, 0, 0)),
        ],
        out_specs=(
            pl.BlockSpec((D, tr), lambda i: (0, i)),
            pl.BlockSpec((H * NK, tr), lambda i: (0, i)),
            pl.BlockSpec((H * NK, tr), lambda i: (0, i)),
            pl.BlockSpec((H * NK, tr), lambda i: (0, i)),
            pl.BlockSpec((H * NK, tr), lambda i: (0, i)),
        ),
        compiler_params=_params(("arbitrary",)),
        name="peer_route",
    )(x2, mod, gn.reshape(1, D), wq, sk)

    td, ec = DENSE_T, DENSE_EC
    assert (ec // NK) % 8 == 0 and (ec // NK) % DENSE_CHUNK == 0
    assert S % td == 0
    rank_spec = pl.BlockSpec((H * NK, td), lambda t, e: (0, t))
    fn_arr = jnp.ones((1, D), F32) if fn is None else fn.reshape(1, D)
    out = pl.pallas_call(
        functools.partial(_dense_kernel, ec_size=ec, chunk=DENSE_CHUNK, n_steps=E // ec,
                          final_norm=fn is not None),
        out_shape=jax.ShapeDtypeStruct((N, D), F32),
        grid=(N // td, E // ec),
        in_specs=[
            pl.BlockSpec((D, td), lambda t, e: (0, t)),
            pl.BlockSpec((ec, D), lambda t, e: (e, 0)),
            pl.BlockSpec((D, ec), lambda t, e: (0, e)),
            rank_spec, rank_spec, rank_spec, rank_spec,
            pl.BlockSpec((td, D), lambda t, e: (t, 0)),
            pl.BlockSpec((1, 6, D), lambda t, e: ((t * td) // S, 0, 0)),
            pl.BlockSpec((1, D), lambda t, e: (0, 0)),
        ],
        out_specs=pl.BlockSpec((td, D), lambda t, e: (t, 0)),
        scratch_shapes=[pltpu.VMEM((D, td), F32), pltpu.VMEM((ec, td), BF16)],
        compiler_params=_params(("arbitrary", "arbitrary")),
        name="peer_experts",
    )(ht, u_bf, vt_bf, cnt, e1, r2, e2, x2, mod, fn_arr)
    return out.reshape(B, S, D)


def kernel(x, c, positions, ada_w, ada_b, norm_mix, norm_ffn, ev_w_in, ev_g_v, ev_w_s, ev_b_s,
           ev_w_pool, ev_pool_scale, ev_w_out, od_w_in, od_lam_q1, od_lam_k1, od_lam_q2,
           od_lam_k2, od_g_sub, od_w_out, peer_w_q, peer_sub_keys, peer_u, peer_v, final_norm):
    B, S, D = x.shape
    depth = ada_w.shape[0]
    inv_freq = 1.0 / (ROPE_THETA ** (jnp.arange(0, DA_QK, 2, dtype=F32) / DA_QK))
    ang = positions.astype(F32)[..., None] * inv_freq
    cos_t = jnp.swapaxes(jnp.cos(ang), 1, 2)
    sin_t = jnp.swapaxes(jnp.sin(ang), 1, 2)
    mods = _ada(c, ada_w, ada_b)
    for l in range(depth):
        mod = mods[l]
        if l % 2 == 0:
            e = l // 2
            x = _even_layer(x, mod, norm_mix[l], ev_w_in[e], ev_g_v[e], ev_w_s[e], ev_b_s[e],
                            ev_w_pool[e], ev_pool_scale[e], ev_w_out[e])
        else:
            o = l // 2
            lam_init = 0.8 - 0.6 * math.exp(-0.3 * l)
            qt, k, vt = _qkv(x, mod, norm_mix[l], od_w_in[o], cos_t, sin_t)
            lam_vecs = jnp.stack([od_lam_q1[o], od_lam_k1[o], od_lam_q2[o], od_lam_k2[o]])
            ot = _attention(qt, k, vt, lam_vecs, od_g_sub[o], lam_init)
            x = _oproj(ot, od_w_out[o], x, mod)
        x = _peer(x, mod, norm_ffn[l], peer_w_q[l], peer_sub_keys[l],
                  peer_u[l].astype(BF16), peer_v[l].T.astype(BF16),
                  fn=final_norm if l == depth - 1 else None)
    return x
```

```python
import functools
import math

import jax
import jax.numpy as jnp
from jax import lax
from jax.experimental import pallas as pl
from jax.experimental.pallas import tpu as pltpu

F32 = jnp.float32
BF16 = jnp.bfloat16
EPS = 1e-6
NEG_INF = float("-inf")

GM_GROUPS = 4
GM_CH = 128
GM_WIDTH = GM_GROUPS * GM_CH
GM_CHUNK = 128
POOL_WINDOWS = (2, 4, 8, 16)
POOL_CH = 128
POOL_WIDTH = len(POOL_WINDOWS) * POOL_CH
POOL_HALO = max(POOL_WINDOWS)
DA_HEADS = 8
DA_QK = 64
DA_V = 128
ROPE_THETA = 10000.0
PEER_HEADS = 8
PEER_NKEYS = 128
PEER_TOPK = 16

V7X_VMEM_LIMIT = 56 * 1024 * 1024

EVEN_TS = 512
QKV_TS = 512
ATT_TQ = 512
ATT_TK = 512
ROUTE_T = 256
DENSE_T = 512
DENSE_SUB = 1024
DENSE_NSUB = 2
DENSE_CHUNK = 2


def _params(sem):
    return pltpu.CompilerParams(dimension_semantics=sem, vmem_limit_bytes=V7X_VMEM_LIMIT)


def _gelu(x):
    return 0.5 * x * (1.0 + lax.erf(x * 0.7071067811865476))


def _norm_mod(x, gn, sc, sh):
    ms = jnp.mean(x * x, axis=-1, keepdims=True)
    return x * lax.rsqrt(ms + EPS) * gn * (1.0 + sc) + sh


def _ada_kernel(c_ref, w_ref, b_ref, o_ref):
    c = c_ref[...]
    ca = c * jax.nn.sigmoid(c)
    o_ref[0, 0] = jnp.dot(ca, w_ref[0], preferred_element_type=F32,
                          precision=lax.Precision.HIGHEST) + b_ref[0, 0]


def _ada(c, ada_w, ada_b):
    L, D, _ = ada_w.shape
    B = c.shape[0]
    out = pl.pallas_call(
        _ada_kernel,
        out_shape=jax.ShapeDtypeStruct((L, 6, B, D), F32),
        grid=(L, 6),
        in_specs=[
            pl.BlockSpec((B, D), lambda l, j: (0, 0)),
            pl.BlockSpec((1, D, D), lambda l, j: (l, 0, j)),
            pl.BlockSpec((1, 1, 1, D), lambda l, j: (l, j, 0, 0)),
        ],
        out_specs=pl.BlockSpec((1, 1, B, D), lambda l, j: (l, j, 0, 0)),
        compiler_params=_params(("arbitrary", "arbitrary")),
        name="ada",
    )(c, ada_w, ada_b.reshape(L, 6, 1, D))
    return jnp.swapaxes(out, 1, 2)


def _even_kernel(x_ref, mod_ref, gn_ref, win_ref, gv_ref, ws_ref, bs_ref, wp_ref, ls_ref,
                 wout_ref, o_ref, ext_ref, mix_ref, *, ts):
    si = pl.program_id(1)
    x = x_ref[0]
    sh1 = mod_ref[0, 0:1, :]
    sc1 = mod_ref[0, 1:2, :]
    g1 = mod_ref[0, 2:3, :]
    h = _norm_mod(x, gn_ref[...], sc1, sh1)
    proj = jnp.dot(h.astype(BF16), win_ref[...], preferred_element_type=F32)

    z = _gelu(proj[:, :2 * GM_WIDTH])
    u = z[:, :GM_WIDTH]
    v = z[:, GM_WIDTH:]
    mu = jnp.mean(v, axis=-1, keepdims=True)
    d = v - mu
    var = jnp.mean(d * d, axis=-1, keepdims=True)
    vn = (d * lax.rsqrt(var + EPS) * gv_ref[...]).astype(BF16)
    r = lax.broadcasted_iota(jnp.int32, (GM_CHUNK, GM_CHUNK), 0)
    cidx = lax.broadcasted_iota(jnp.int32, (GM_CHUNK, GM_CHUNK), 1)
    causal = r >= cidx
    for g in range(GM_GROUPS):
        wg = jnp.where(causal, ws_ref[g], 0.0).astype(BF16)
        for c in range(ts // GM_CHUNK):
            rs = slice(c * GM_CHUNK, (c + 1) * GM_CHUNK)
            cs = slice(g * GM_CH, (g + 1) * GM_CH)
            sv = jnp.dot(wg, vn[rs, cs], preferred_element_type=F32) + bs_ref[g]
            mix_ref[rs, cs] = (u[rs, cs] * sv).astype(BF16)

    p = proj[:, 2 * GM_WIDTH:]

    @pl.when(si == 0)
    def _():
        ext_ref[0:POOL_HALO, :] = jnp.zeros((POOL_HALO, POOL_WIDTH), F32)

    ext_ref[POOL_HALO:POOL_HALO + ts, :] = p
    t = si * ts + lax.broadcasted_iota(jnp.int32, (ts, POOL_CH), 0)
    for g, w in enumerate(POOL_WINDOWS):
        cs = slice(g * POOL_CH, (g + 1) * POOL_CH)
        acc = ext_ref[POOL_HALO:POOL_HALO + ts, cs]
        for k in range(1, w):
            acc = acc + ext_ref[POOL_HALO - k:POOL_HALO - k + ts, cs]
        cnt = jnp.minimum(t + 1, w).astype(F32)
        pooled = acc / cnt - p[:, cs]
        yb = jnp.dot(pooled.astype(BF16), wp_ref[g], preferred_element_type=F32) * ls_ref[:, cs]
        mix_ref[:, GM_WIDTH + g * POOL_CH:GM_WIDTH + (g + 1) * POOL_CH] = yb.astype(BF16)
    ext_ref[0:POOL_HALO, :] = ext_ref[ts:ts + POOL_HALO, :]

    y = jnp.dot(mix_ref[...], wout_ref[...], preferred_element_type=F32)
    o_ref[0] = x + g1 * y


def _even_layer(x, mod, gn, w_in, g_v, w_s, b_s, w_pool, ls, w_out):
    B, S, D = x.shape
    ts = EVEN_TS
    n_in = w_in.shape[1]
    bsb = jnp.broadcast_to(b_s[:, :, None], (GM_GROUPS, GM_CHUNK, GM_CH))
    return pl.pallas_call(
        functools.partial(_even_kernel, ts=ts),
        out_shape=jax.ShapeDtypeStruct((B, S, D), F32),
        grid=(B, S // ts),
        in_specs=[
            pl.BlockSpec((1, ts, D), lambda b, s: (b, s, 0)),
            pl.BlockSpec((1, 6, D), lambda b, s: (b, 0, 0)),
            pl.BlockSpec((1, D), lambda b, s: (0, 0)),
            pl.BlockSpec((D, n_in), lambda b, s: (0, 0)),
            pl.BlockSpec((1, GM_WIDTH), lambda b, s: (0, 0)),
            pl.BlockSpec((GM_GROUPS, GM_CHUNK, GM_CHUNK), lambda b, s: (0, 0, 0)),
            pl.BlockSpec((GM_GROUPS, GM_CHUNK, GM_CH), lambda b, s: (0, 0, 0)),
            pl.BlockSpec((len(POOL_WINDOWS), POOL_CH, POOL_CH), lambda b, s: (0, 0, 0)),
            pl.BlockSpec((1, POOL_WIDTH), lambda b, s: (0, 0)),
            pl.BlockSpec((GM_WIDTH + POOL_WIDTH, D), lambda b, s: (0, 0)),
        ],
        out_specs=pl.BlockSpec((1, ts, D), lambda b, s: (b, s, 0)),
        scratch_shapes=[
            pltpu.VMEM((POOL_HALO + ts, POOL_WIDTH), F32),
            pltpu.VMEM((ts, GM_WIDTH + POOL_WIDTH), BF16),
        ],
        compiler_params=_params(("arbitrary", "arbitrary")),
        name="even_mixer",
    )(x, mod, gn.reshape(1, D), w_in.astype(BF16), g_v.reshape(1, -1), w_s, bsb,
      w_pool.astype(BF16), ls.reshape(1, -1), w_out.astype(BF16))


def _qkv_kernel(x_ref, mod_ref, gn_ref, wt_ref, cos_ref, sin_ref, qt_ref, k_ref, vt_ref,
                kt_scr, *, d_model):
    x = x_ref[0]
    sh1 = mod_ref[0, 0:1, :]
    sc1 = mod_ref[0, 1:2, :]
    h = _norm_mod(x, gn_ref[...], sc1, sh1)
    ht = h.T.astype(BF16)
    pt = jnp.dot(wt_ref[...], ht, preferred_element_type=F32)
    cos = cos_ref[0]
    sin = sin_ref[0]
    half = DA_QK // 2
    scale = DA_QK ** -0.5 * math.log2(math.e)
    for g in range(2 * DA_HEADS):
        r0 = g * DA_QK
        t1 = pt[r0:r0 + half]
        t2 = pt[r0 + half:r0 + DA_QK]
        qt_ref[0, r0:r0 + half, :] = ((t1 * cos - t2 * sin) * scale).astype(BF16)
        qt_ref[0, r0 + half:r0 + DA_QK, :] = ((t2 * cos + t1 * sin) * scale).astype(BF16)
        k0 = d_model + r0
        t1 = pt[k0:k0 + half]
        t2 = pt[k0 + half:k0 + DA_QK]
        kt_scr[r0:r0 + half, :] = t1 * cos - t2 * sin
        kt_scr[r0 + half:r0 + DA_QK, :] = t2 * cos + t1 * sin
    k_ref[0] = kt_scr[...].T.astype(BF16)
    vt_ref[0] = pt[2 * d_model:3 * d_model].astype(BF16)


def _qkv(x, mod, gn, w_in, cos_t, sin_t):
    B, S, D = x.shape
    ts = QKV_TS
    wt = w_in.T.astype(BF16)
    return pl.pallas_call(
        functools.partial(_qkv_kernel, d_model=D),
        out_shape=(
            jax.ShapeDtypeStruct((B, D, S), BF16),
            jax.ShapeDtypeStruct((B, S, D), BF16),
            jax.ShapeDtypeStruct((B, D, S), BF16),
        ),
        grid=(B, S // ts),
        in_specs=[
            pl.BlockSpec((1, ts, D), lambda b, s: (b, s, 0)),
            pl.BlockSpec((1, 6, D), lambda b, s: (b, 0, 0)),
            pl.BlockSpec((1, D), lambda b, s: (0, 0)),
            pl.BlockSpec((3 * D, D), lambda b, s: (0, 0)),
            pl.BlockSpec((1, DA_QK // 2, ts), lambda b, s: (b, 0, s)),
            pl.BlockSpec((1, DA_QK // 2, ts), lambda b, s: (b, 0, s)),
        ],
        out_specs=(
            pl.BlockSpec((1, D, ts), lambda b, s: (b, 0, s)),
            pl.BlockSpec((1, ts, D), lambda b, s: (b, s, 0)),
            pl.BlockSpec((1, D, ts), lambda b, s: (b, 0, s)),
        ),
        scratch_shapes=[pltpu.VMEM((D, ts), F32)],
        compiler_params=_params(("arbitrary", "arbitrary")),
        name="qkv_rope",
    )(x, mod, gn.reshape(1, D), wt, cos_t, sin_t)


def _attn_kernel(qt_ref, k_ref, vt_ref, lam_ref, gs_ref, o_ref, *, tq, tk, lam_init):
    qi = pl.program_id(2)
    qt = qt_ref[0]
    row = lax.broadcasted_iota(jnp.int32, qt.shape, 0)
    zero = jnp.zeros_like(qt)
    q12 = jnp.concatenate([jnp.where(row < DA_QK, qt, zero), jnp.where(row >= DA_QK, qt, zero)],
                          axis=1)

    def update(j, masked, carry):
        m_old, l_old, acc_old = carry
        off = pl.multiple_of(j * tk, tk)
        kb = k_ref[0, pl.ds(off, tk), :]
        vb = vt_ref[0, :, pl.ds(off, tk)]
        s = jnp.dot(kb, q12, preferred_element_type=F32)
        if masked:
            kpos = off + lax.broadcasted_iota(jnp.int32, s.shape, 0)
            lane = lax.broadcasted_iota(jnp.int32, s.shape, 1)
            qpos = qi * tq + jnp.where(lane >= tq, lane - tq, lane)
            s = jnp.where(kpos <= qpos, s, NEG_INF)
        m_new = jnp.maximum(m_old, jnp.max(s, axis=0, keepdims=True))
        alpha = jnp.exp2(m_old - m_new)
        p = jnp.exp2(s - m_new)
        l_new = alpha * l_old + jnp.sum(p, axis=0, keepdims=True)
        acc_new = alpha * acc_old + jnp.dot(vb, p.astype(BF16), preferred_element_type=F32)
        return m_new, l_new, acc_new

    init = (jnp.full((1, 2 * tq), NEG_INF, F32), jnp.zeros((1, 2 * tq), F32),
            jnp.zeros((DA_V, 2 * tq), F32))
    n_full = (qi * tq) // tk

    def body(jj, carry):
        carry = update(2 * jj, False, carry)
        return update(2 * jj + 1, False, carry)

    carry = lax.fori_loop(0, n_full // 2, body, init)
    carry = lax.cond(n_full % 2 == 1, lambda c: update(n_full - 1, False, c), lambda c: c, carry)
    _, l, acc = update(n_full, True, carry)

    lv = lam_ref[...]
    la = jnp.sum(lv[0:1] * lv[1:2], axis=-1, keepdims=True)
    lb = jnp.sum(lv[2:3] * lv[3:4], axis=-1, keepdims=True)
    lam = jnp.exp(la) - jnp.exp(lb) + lam_init
    o = acc[:, :tq] / l[:, :tq] - lam * (acc[:, tq:] / l[:, tq:])
    ms = jnp.mean(o * o, axis=0, keepdims=True)
    o = o * lax.rsqrt(ms + EPS) * gs_ref[...] * (1.0 - lam_init)
    o_ref[0] = o.astype(BF16)


def _attention(qt, k, vt, lam_vecs, g_sub, lam_init):
    B, D, S = qt.shape
    tq, tk = ATT_TQ, ATT_TK
    assert tk % tq == 0 and S % tk == 0
    hd = 2 * DA_QK
    return pl.pallas_call(
        functools.partial(_attn_kernel, tq=tq, tk=tk, lam_init=lam_init),
        out_shape=jax.ShapeDtypeStruct((B, D, S), BF16),
        grid=(B, DA_HEADS, S // tq),
        in_specs=[
            pl.BlockSpec((1, hd, tq), lambda b, h, q: (b, h, q)),
            pl.BlockSpec((1, S, hd), lambda b, h, q: (b, 0, h)),
            pl.BlockSpec((1, DA_V, S), lambda b, h, q: (b, h, 0)),
            pl.BlockSpec((4, DA_QK), lambda b, h, q: (0, 0)),
            pl.BlockSpec((DA_V, 1), lambda b, h, q: (0, 0)),
        ],
        out_specs=pl.BlockSpec((1, DA_V, tq), lambda b, h, q: (b, h, q)),
        compiler_params=_params(("arbitrary", "arbitrary", "arbitrary")),
        name="diff_attention",
    )(qt, k, vt, lam_vecs, g_sub.reshape(DA_V, 1))


def _oproj_kernel(ot_ref, wt_ref, x_ref, mod_ref, o_ref):
    yt = jnp.dot(wt_ref[...], ot_ref[0], preferred_element_type=F32)
    g1 = mod_ref[0, 2:3, :]
    o_ref[0] = x_ref[0] + g1 * yt.T


def _oproj(ot, w_out, x, mod):
    B, S, D = x.shape
    ts = QKV_TS
    return pl.pallas_call(
        _oproj_kernel,
        out_shape=jax.ShapeDtypeStruct((B, S, D), F32),
        grid=(B, S // ts),
        in_specs=[
            pl.BlockSpec((1, D, ts), lambda b, s: (b, 0, s)),
            pl.BlockSpec((D, D), lambda b, s: (0, 0)),
            pl.BlockSpec((1, ts, D), lambda b, s: (b, s, 0)),
            pl.BlockSpec((1, 6, D), lambda b, s: (b, 0, 0)),
        ],
        out_specs=pl.BlockSpec((1, ts, D), lambda b, s: (b, s, 0)),
        compiler_params=_params(("arbitrary", "arbitrary")),
        name="attn_out_proj",
    )(ot, w_out.T.astype(BF16), x, mod)


def _top_values(s, k):
    vals = []
    for _ in range(k):
        m = jnp.max(s, axis=0, keepdims=True)
        vals.append(m)
        s = jnp.where(s == m, NEG_INF, s)
    return vals


def _top_values_128(s, k):
    n = s.shape[0] // 8
    assert n == 16 and k <= n
    blocks = [s[8 * v:8 * v + 8, :] for v in range(n)]
    size = 2
    while size <= n:
        stride = size // 2
        while stride >= 1:
            for i in range(n):
                j = i ^ stride
                if j > i:
                    hi = jnp.maximum(blocks[i], blocks[j])
                    lo = jnp.minimum(blocks[i], blocks[j])
                    blocks[i], blocks[j] = (hi, lo) if (i & size) == 0 else (lo, hi)
            stride //= 2
        size *= 2
    vals = []
    for t in range(k):
        head = blocks[0]
        m = jnp.max(head, axis=0, keepdims=True)
        vals.append(m)
        hit = head == m
        for r in range(k - 1 - t):
            blocks[r] = jnp.where(hit, blocks[r + 1], blocks[r])
    return vals


def _route_kernel(x_ref, mod_ref, gn_ref, wq_ref, sk_ref, ht_ref, cnt_ref, e1_ref, r2_ref, e2_ref):
    x = x_ref[...]
    sh2 = mod_ref[0, 3:4, :]
    sc2 = mod_ref[0, 4:5, :]
    h = _norm_mod(x, gn_ref[...], sc2, sh2)
    ht_ref[...] = h.T.astype(BF16)
    q = jnp.dot(h.astype(BF16), wq_ref[...], preferred_element_type=F32)
    K = PEER_TOPK
    nk = PEER_NKEYS
    for hd in range(PEER_HEADS):
        rs = slice(hd * nk, (hd + 1) * nk)
        s12 = []
        for p in range(2):
            c0 = (hd * 2 + p) * nk
            qhp = q[:, c0:c0 + nk].astype(BF16)
            s12.append(lax.dot_general(sk_ref[hd * 2 + p], qhp, (((1,), (1,)), ((), ())),
                                       preferred_element_type=F32))
        s1, s2 = s12
        a = _top_values_128(s1, K)
        b = _top_values_128(s2, K)
        r2 = jnp.full(s2.shape, float(K), F32)
        for l in reversed(range(K)):
            r2 = jnp.where(s2 >= b[l], float(l), r2)
        b_lo = jnp.concatenate(b[:8], axis=0)
        b_hi = jnp.concatenate(b[8:], axis=0)
        a_hi = jnp.concatenate(a[8:], axis=0)
        cand = [a[k] + b_lo for k in range(8)] + [a[0] + b_hi, a_hi + b[0]]
        c = _top_values(jnp.concatenate(cand, axis=0), K)
        tau = c[K - 1]
        zsum = jnp.ones_like(c[0])
        for k in range(1, K):
            zsum = zsum + jnp.exp(c[k] - c[0])
        a_all = jnp.concatenate(a, axis=0)
        ck = jnp.zeros(a_all.shape, F32)
        for l in range(K):
            ck = ck + jnp.where((a_all + b[l]) >= tau, 1.0, 0.0)
        cnt = jnp.zeros(s1.shape, F32)
        for k in range(K):
            cnt = jnp.where(s1 == a[k], ck[k:k + 1, :], cnt)
        cnt_ref[rs, :] = cnt
        e1_ref[rs, :] = jnp.exp(s1 - a[0])
        r2_ref[rs, :] = r2.astype(BF16)
        e2_ref[rs, :] = (jnp.exp(s2 - b[0]) / zsum).astype(BF16)


def _dense_kernel(ht_ref, u_ref, vt_ref, cnt_ref, e1_ref, r2_ref, e2_ref, x_ref, mod_ref, fn_ref,
                  o_ref, acc_scr, *wa_scrs, sub_size, chunk, n_steps, final_norm):
    ec = pl.program_id(1)
    nk = PEER_NKEYS
    per = sub_size // nk
    group = 8
    T = ht_ref.shape[1]

    @pl.when(ec == 0)
    def _():
        acc_scr[...] = jnp.zeros(acc_scr.shape, F32)

    def gate_block(i_rel):
        w = jnp.zeros((nk, T), BF16)
        grp, ii = divmod(i_rel, group)
        for hd in range(PEER_HEADS):
            rs = slice(hd * nk, (hd + 1) * nk)
            base = pl.multiple_of(hd * nk + ec * (len(wa_scrs) * per) + grp * group, group)
            cnt_row = cnt_ref[pl.ds(base, group), :][ii:ii + 1, :]
            e1_row = e1_ref[pl.ds(base, group), :][ii:ii + 1, :]
            cb = jnp.broadcast_to(cnt_row, (nk, T)).astype(BF16)
            eb = jnp.broadcast_to(e1_row, (nk, T)).astype(BF16)
            sel = r2_ref[rs, :] < cb
            w = w + jnp.where(sel, e2_ref[rs, :], jnp.zeros((), BF16)) * eb
        return w

    total = None
    for sidx, wa_scr in enumerate(wa_scrs):
        for c in range(per // chunk):
            rows = slice(c * chunk * nk, (c + 1) * chunk * nk)
            urows = slice(sidx * sub_size + c * chunk * nk, sidx * sub_size + (c + 1) * chunk * nk)
            act = jnp.dot(u_ref[urows, :], ht_ref[...], preferred_element_type=F32)
            g = _gelu(act.astype(BF16))
            w = jnp.concatenate([gate_block(sidx * per + c * chunk + j) for j in range(chunk)],
                                axis=0)
            wa_scr[rows, :] = w * g
        y = jnp.dot(vt_ref[:, sidx * sub_size:(sidx + 1) * sub_size], wa_scr[...],
                    preferred_element_type=F32)
        total = y if total is None else total + y
    acc_scr[...] += total

    @pl.when(ec == n_steps - 1)
    def _():
        g2 = mod_ref[0, 5:6, :]
        y = x_ref[...] + g2 * acc_scr[...].T
        if final_norm:
            ms = jnp.mean(y * y, axis=-1, keepdims=True)
            y = y * lax.rsqrt(ms + EPS) * fn_ref[...]
        o_ref[...] = y


def _peer(x, mod, gn, w_q, sub_keys, u_bf, vt_bf, fn=None):
    B, S, D = x.shape
    N = B * S
    H, NK = PEER_HEADS, PEER_NKEYS
    E = u_bf.shape[0]
    tr = ROUTE_T
    x2 = x.reshape(N, D)
    wq = w_q.reshape(D, H * 2 * NK).astype(BF16)
    sk = sub_keys.reshape(H * 2, NK, NK).astype(BF16)
    ht, cnt, e1, r2, e2 = pl.pallas_call(
        _route_kernel,
        out_shape=(
            jax.ShapeDtypeStruct((D, N), BF16),
            jax.ShapeDtypeStruct((H * NK, N), F32),
            jax.ShapeDtypeStruct((H * NK, N), F32),
            jax.ShapeDtypeStruct((H * NK, N), BF16),
            jax.ShapeDtypeStruct((H * NK, N), BF16),
        ),
        grid=(N // tr,),
        in_specs=[
            pl.BlockSpec((tr, D), lambda i: (i, 0)),
            pl.BlockSpec((1, 6, D), lambda i: ((i * tr) // S, 0, 0)),
            pl.BlockSpec((1, D), lambda i: (0, 0)),
            pl.BlockSpec((D, H * 2 * NK), lambda i: (0, 0)),
            pl.BlockSpec((H * 2, NK, NK), lambda i: (0, 0, 0)),
        ],
        out_specs=(
            pl.BlockSpec((D, tr), lambda i: (0, i)),
            pl.BlockSpec((H * NK, tr), lambda i: (0, i)),
            pl.BlockSpec((H * NK, tr), lambda i: (0, i)),
            pl.BlockSpec((H * NK, tr), lambda i: (0, i)),
            pl.BlockSpec((H * NK, tr), lambda i: (0, i)),
        ),
        compiler_params=_params(("arbitrary",)),
        name="peer_route",
    )(x2, mod, gn.reshape(1, D), wq, sk)

    td, sub, nsub = DENSE_T, DENSE_SUB, DENSE_NSUB
    ec = sub * nsub
    assert (sub // NK) % 8 == 0 and (sub // NK) % DENSE_CHUNK == 0 and E % ec == 0
    assert S % td == 0
    rank_spec = pl.BlockSpec((H * NK, td), lambda t, e: (0, t))
    fn_arr = jnp.ones((1, D), F32) if fn is None else fn.reshape(1, D)
    out = pl.pallas_call(
        functools.partial(_dense_kernel, sub_size=sub, chunk=DENSE_CHUNK, n_steps=E // ec,
                          final_norm=fn is not None),
        out_shape=jax.ShapeDtypeStruct((N, D), F32),
        grid=(N // td, E // ec),
        in_specs=[
            pl.BlockSpec((D, td), lambda t, e: (0, t)),
            pl.BlockSpec((ec, D), lambda t, e: (e, 0)),
            pl.BlockSpec((D, ec), lambda t, e: (0, e)),
            rank_spec, rank_spec, rank_spec, rank_spec,
            pl.BlockSpec((td, D), lambda t, e: (t, 0)),
            pl.BlockSpec((1, 6, D), lambda t, e: ((t * td) // S, 0, 0)),
            pl.BlockSpec((1, D), lambda t, e: (0, 0)),
        ],
        out_specs=pl.BlockSpec((td, D), lambda t, e: (t, 0)),
        scratch_shapes=[pltpu.VMEM((D, td), F32)] + [pltpu.VMEM((sub, td), BF16)] * nsub,
        compiler_params=_params(("arbitrary", "arbitrary")),
        name="peer_experts",
    )(ht, u_bf, vt_bf, cnt, e1, r2, e2, x2, mod, fn_arr)
    return out.reshape(B, S, D)


def kernel(x, c, positions, ada_w, ada_b, norm_mix, norm_ffn, ev_w_in, ev_g_v, ev_w_s, ev_b_s,
           ev_w_pool, ev_pool_scale, ev_w_out, od_w_in, od_lam_q1, od_lam_k1, od_lam_q2,
           od_lam_k2, od_g_sub, od_w_out, peer_w_q, peer_sub_keys, peer_u, peer_v, final_norm):
    B, S, D = x.shape
    depth = ada_w.shape[0]
    inv_freq = 1.0 / (ROPE_THETA ** (jnp.arange(0, DA_QK, 2, dtype=F32) / DA_QK))
    ang = positions.astype(F32)[..., None] * inv_freq
    cos_t = jnp.swapaxes(jnp.cos(ang), 1, 2)
    sin_t = jnp.swapaxes(jnp.sin(ang), 1, 2)
    mods = _ada(c, ada_w, ada_b)
    for l in range(depth):
        mod = mods[l]
        if l % 2 == 0:
            e = l // 2
            x = _even_layer(x, mod, norm_mix[l], ev_w_in[e], ev_g_v[e], ev_w_s[e], ev_b_s[e],
                            ev_w_pool[e], ev_pool_scale[e], ev_w_out[e])
        else:
            o = l // 2
            lam_init = 0.8 - 0.6 * math.exp(-0.3 * l)
            qt, k, vt = _qkv(x, mod, norm_mix[l], od_w_in[o], cos_t, sin_t)
            lam_vecs = jnp.stack([od_lam_q1[o], od_lam_k1[o], od_lam_q2[o], od_lam_k2[o]])
            ot = _attention(qt, k, vt, lam_vecs, od_g_sub[o], lam_init)
            x = _oproj(ot, od_w_out[o], x, mod)
        x = _peer(x, mod, norm_ffn[l], peer_w_q[l], peer_sub_keys[l],
                  peer_u[l].astype(BF16), peer_v[l].T.astype(BF16),
                  fn=final_norm if l == depth - 1 else None)
    return x
```

```python
import functools
import math

import jax
import jax.numpy as jnp
from jax import lax
from jax.experimental import pallas as pl
from jax.experimental.pallas import tpu as pltpu

F32 = jnp.float32
BF16 = jnp.bfloat16
EPS = 1e-6
NEG_INF = float("-inf")

GM_GROUPS = 4
GM_CH = 128
GM_WIDTH = GM_GROUPS * GM_CH
GM_CHUNK = 128
POOL_WINDOWS = (2, 4, 8, 16)
POOL_CH = 128
POOL_WIDTH = len(POOL_WINDOWS) * POOL_CH
POOL_HALO = max(POOL_WINDOWS)
DA_HEADS = 8
DA_QK = 64
DA_V = 128
ROPE_THETA = 10000.0
PEER_HEADS = 8
PEER_NKEYS = 128
PEER_TOPK = 16

F32_SUBLANES = 8

V7X_VMEM_LIMIT = 56 * 1024 * 1024

EVEN_TS = 512
QKV_TS = 512
ATT_TQ = 512
ATT_TK = 512
ATT_TK_BIG = 1024
ROUTE_T = 256
DENSE_T = 512
DENSE_SUB = 1024
DENSE_NSUB = 2
DENSE_CHUNK = 2


def _params(sem):
    return pltpu.CompilerParams(dimension_semantics=sem, vmem_limit_bytes=V7X_VMEM_LIMIT)


def _gelu(x):
    return 0.5 * x * (1.0 + lax.erf(x * 0.7071067811865476))


def _norm_mod(x, gn, sc, sh):
    ms = jnp.mean(x * x, axis=-1, keepdims=True)
    return x * lax.rsqrt(ms + EPS) * gn * (1.0 + sc) + sh


def _ada_kernel(c_ref, w_ref, b_ref, o_ref):
    c = c_ref[...]
    ca = c * jax.nn.sigmoid(c)
    o_ref[0, 0] = jnp.dot(ca, w_ref[0], preferred_element_type=F32,
                          precision=lax.Precision.HIGHEST) + b_ref[0, 0]


def _ada(c, ada_w, ada_b):
    L, D, _ = ada_w.shape
    B = c.shape[0]
    out = pl.pallas_call(
        _ada_kernel,
        out_shape=jax.ShapeDtypeStruct((L, 6, B, D), F32),
        grid=(L, 6),
        in_specs=[
            pl.BlockSpec((B, D), lambda l, j: (0, 0)),
            pl.BlockSpec((1, D, D), lambda l, j: (l, 0, j)),
            pl.BlockSpec((1, 1, 1, D), lambda l, j: (l, j, 0, 0)),
        ],
        out_specs=pl.BlockSpec((1, 1, B, D), lambda l, j: (l, j, 0, 0)),
        compiler_params=_params(("arbitrary", "arbitrary")),
        name="ada",
    )(c, ada_w, ada_b.reshape(L, 6, 1, D))
    return jnp.swapaxes(out, 1, 2)


def _even_kernel(x_ref, mod_ref, gn_ref, win_ref, gv_ref, ws_ref, bs_ref, wp_ref, ls_ref,
                 wout_ref, o_ref, ext_ref, mix_ref, *, ts):
    si = pl.program_id(1)
    x = x_ref[0]
    sh1 = mod_ref[0, 0:1, :]
    sc1 = mod_ref[0, 1:2, :]
    g1 = mod_ref[0, 2:3, :]
    h = _norm_mod(x, gn_ref[...], sc1, sh1)
    proj = jnp.dot(h.astype(BF16), win_ref[...], preferred_element_type=F32)

    z = _gelu(proj[:, :2 * GM_WIDTH])
    u = z[:, :GM_WIDTH]
    v = z[:, GM_WIDTH:]
    mu = jnp.mean(v, axis=-1, keepdims=True)
    d = v - mu
    var = jnp.mean(d * d, axis=-1, keepdims=True)
    vn = (d * lax.rsqrt(var + EPS) * gv_ref[...]).astype(BF16)
    r = lax.broadcasted_iota(jnp.int32, (GM_CHUNK, GM_CHUNK), 0)
    cidx = lax.broadcasted_iota(jnp.int32, (GM_CHUNK, GM_CHUNK), 1)
    causal = r >= cidx
    for g in range(GM_GROUPS):
        wg = jnp.where(causal, ws_ref[g], 0.0).astype(BF16)
        for c in range(ts // GM_CHUNK):
            rs = slice(c * GM_CHUNK, (c + 1) * GM_CHUNK)
            cs = slice(g * GM_CH, (g + 1) * GM_CH)
            sv = jnp.dot(wg, vn[rs, cs], preferred_element_type=F32) + bs_ref[g]
            mix_ref[rs, cs] = (u[rs, cs] * sv).astype(BF16)

    p = proj[:, 2 * GM_WIDTH:]

    @pl.when(si == 0)
    def _():
        ext_ref[0:POOL_HALO, :] = jnp.zeros((POOL_HALO, POOL_WIDTH), F32)

    ext_ref[POOL_HALO:POOL_HALO + ts, :] = p
    t = si * ts + lax.broadcasted_iota(jnp.int32, (ts, POOL_CH), 0)
    for g, w in enumerate(POOL_WINDOWS):
        cs = slice(g * POOL_CH, (g + 1) * POOL_CH)
        acc = ext_ref[POOL_HALO:POOL_HALO + ts, cs]
        for k in range(1, w):
            acc = acc + ext_ref[POOL_HALO - k:POOL_HALO - k + ts, cs]
        cnt = jnp.minimum(t + 1, w).astype(F32)
        pooled = acc / cnt - p[:, cs]
        yb = jnp.dot(pooled.astype(BF16), wp_ref[g], preferred_element_type=F32) * ls_ref[:, cs]
        mix_ref[:, GM_WIDTH + g * POOL_CH:GM_WIDTH + (g + 1) * POOL_CH] = yb.astype(BF16)
    ext_ref[0:POOL_HALO, :] = ext_ref[ts:ts + POOL_HALO, :]

    y = jnp.dot(mix_ref[...], wout_ref[...], preferred_element_type=F32)
    o_ref[0] = x + g1 * y


def _even_layer(x, mod, gn, w_in, g_v, w_s, b_s, w_pool, ls, w_out):
    B, S, D = x.shape
    ts = EVEN_TS
    n_in = w_in.shape[1]
    bsb = jnp.broadcast_to(b_s[:, :, None], (GM_GROUPS, GM_CHUNK, GM_CH))
    return pl.pallas_call(
        functools.partial(_even_kernel, ts=ts),
        out_shape=jax.ShapeDtypeStruct((B, S, D), F32),
        grid=(B, S // ts),
        in_specs=[
            pl.BlockSpec((1, ts, D), lambda b, s: (b, s, 0)),
            pl.BlockSpec((1, 6, D), lambda b, s: (b, 0, 0)),
            pl.BlockSpec((1, D), lambda b, s: (0, 0)),
            pl.BlockSpec((D, n_in), lambda b, s: (0, 0)),
            pl.BlockSpec((1, GM_WIDTH), lambda b, s: (0, 0)),
            pl.BlockSpec((GM_GROUPS, GM_CHUNK, GM_CHUNK), lambda b, s: (0, 0, 0)),
            pl.BlockSpec((GM_GROUPS, GM_CHUNK, GM_CH), lambda b, s: (0, 0, 0)),
            pl.BlockSpec((len(POOL_WINDOWS), POOL_CH, POOL_CH), lambda b, s: (0, 0, 0)),
            pl.BlockSpec((1, POOL_WIDTH), lambda b, s: (0, 0)),
            pl.BlockSpec((GM_WIDTH + POOL_WIDTH, D), lambda b, s: (0, 0)),
        ],
        out_specs=pl.BlockSpec((1, ts, D), lambda b, s: (b, s, 0)),
        scratch_shapes=[
            pltpu.VMEM((POOL_HALO + ts, POOL_WIDTH), F32),
            pltpu.VMEM((ts, GM_WIDTH + POOL_WIDTH), BF16),
        ],
        compiler_params=_params(("arbitrary", "arbitrary")),
        name="even_mixer",
    )(x, mod, gn.reshape(1, D), w_in.astype(BF16), g_v.reshape(1, -1), w_s, bsb,
      w_pool.astype(BF16), ls.reshape(1, -1), w_out.astype(BF16))


def _qkv_kernel(x_ref, mod_ref, gn_ref, wt_ref, cos_ref, sin_ref, qt_ref, k_ref, vt_ref,
                kt_scr, *, d_model):
    x = x_ref[0]
    sh1 = mod_ref[0, 0:1, :]
    sc1 = mod_ref[0, 1:2, :]
    h = _norm_mod(x, gn_ref[...], sc1, sh1)
    ht = h.T.astype(BF16)
    pt = jnp.dot(wt_ref[...], ht, preferred_element_type=F32)
    cos = cos_ref[0]
    sin = sin_ref[0]
    half = DA_QK // 2
    scale = DA_QK ** -0.5 * math.log2(math.e)
    for g in range(2 * DA_HEADS):
        r0 = g * DA_QK
        t1 = pt[r0:r0 + half]
        t2 = pt[r0 + half:r0 + DA_QK]
        qt_ref[0, r0:r0 + half, :] = ((t1 * cos - t2 * sin) * scale).astype(BF16)
        qt_ref[0, r0 + half:r0 + DA_QK, :] = ((t2 * cos + t1 * sin) * scale).astype(BF16)
        k0 = d_model + r0
        t1 = pt[k0:k0 + half]
        t2 = pt[k0 + half:k0 + DA_QK]
        kt_scr[r0:r0 + half, :] = t1 * cos - t2 * sin
        kt_scr[r0 + half:r0 + DA_QK, :] = t2 * cos + t1 * sin
    k_ref[0] = kt_scr[...].T.astype(BF16)
    vt_ref[0] = pt[2 * d_model:3 * d_model].astype(BF16)


def _qkv(x, mod, gn, w_in, cos_t, sin_t):
    B, S, D = x.shape
    ts = QKV_TS
    wt = w_in.T.astype(BF16)
    return pl.pallas_call(
        functools.partial(_qkv_kernel, d_model=D),
        out_shape=(
            jax.ShapeDtypeStruct((B, D, S), BF16),
            jax.ShapeDtypeStruct((B, S, D), BF16),
            jax.ShapeDtypeStruct((B, D, S), BF16),
        ),
        grid=(B, S // ts),
        in_specs=[
            pl.BlockSpec((1, ts, D), lambda b, s: (b, s, 0)),
            pl.BlockSpec((1, 6, D), lambda b, s: (b, 0, 0)),
            pl.BlockSpec((1, D), lambda b, s: (0, 0)),
            pl.BlockSpec((3 * D, D), lambda b, s: (0, 0)),
            pl.BlockSpec((1, DA_QK // 2, ts), lambda b, s: (b, 0, s)),
            pl.BlockSpec((1, DA_QK // 2, ts), lambda b, s: (b, 0, s)),
        ],
        out_specs=(
            pl.BlockSpec((1, D, ts), lambda b, s: (b, 0, s)),
            pl.BlockSpec((1, ts, D), lambda b, s: (b, s, 0)),
            pl.BlockSpec((1, D, ts), lambda b, s: (b, 0, s)),
        ),
        scratch_shapes=[pltpu.VMEM((D, ts), F32)],
        compiler_params=_params(("arbitrary", "arbitrary")),
        name="qkv_rope",
    )(x, mod, gn.reshape(1, D), wt, cos_t, sin_t)


def _attn_kernel(qt_ref, k_ref, vt_ref, lam_ref, gs_ref, o_ref, *, tq, tk, tk_big, lam_init):
    qi = pl.program_id(2)
    qt = qt_ref[0]
    row = lax.broadcasted_iota(jnp.int32, qt.shape, 0)
    zero = jnp.zeros_like(qt)
    q12 = jnp.concatenate([jnp.where(row < DA_QK, qt, zero), jnp.where(row >= DA_QK, qt, zero)],
                          axis=1)

    def update(start, size, masked, carry):
        m_old, l_old, acc_old = carry
        off = pl.multiple_of(start, tk)
        kb = k_ref[0, pl.ds(off, size), :]
        vb = vt_ref[0, :, pl.ds(off, size)]
        s = jnp.dot(kb, q12, preferred_element_type=F32)
        if masked:
            kpos = off + lax.broadcasted_iota(jnp.int32, s.shape, 0)
            lane = lax.broadcasted_iota(jnp.int32, s.shape, 1)
            qpos = qi * tq + jnp.where(lane >= tq, lane - tq, lane)
            s = jnp.where(kpos <= qpos, s, NEG_INF)
        m_new = jnp.maximum(m_old, jnp.max(s, axis=0, keepdims=True))
        alpha = jnp.exp2(m_old - m_new)
        p = jnp.exp2(s - m_new)
        l_new = alpha * l_old + jnp.sum(p, axis=0, keepdims=True)
        acc_new = alpha * acc_old + jnp.dot(vb, p.astype(BF16), preferred_element_type=F32)
        return m_new, l_new, acc_new

    init = (jnp.full((1, 2 * tq), NEG_INF, F32), jnp.zeros((1, 2 * tq), F32),
            jnp.zeros((DA_V, 2 * tq), F32))
    visible = qi * tq
    n_big = visible // tk_big
    carry = lax.fori_loop(0, n_big, lambda j, c: update(j * tk_big, tk_big, False, c), init)
    carry = lax.cond(visible - n_big * tk_big >= tk,
                     lambda c: update(n_big * tk_big, tk, False, c), lambda c: c, carry)
    _, l, acc = update(visible, tk, True, carry)

    lv = lam_ref[...]
    la = jnp.sum(lv[0:1] * lv[1:2], axis=-1, keepdims=True)
    lb = jnp.sum(lv[2:3] * lv[3:4], axis=-1, keepdims=True)
    lam = jnp.exp(la) - jnp.exp(lb) + lam_init
    o = acc[:, :tq] / l[:, :tq] - lam * (acc[:, tq:] / l[:, tq:])
    ms = jnp.mean(o * o, axis=0, keepdims=True)
    o = o * lax.rsqrt(ms + EPS) * gs_ref[...] * (1.0 - lam_init)
    o_ref[0] = o.astype(BF16)


def _attention(qt, k, vt, lam_vecs, g_sub, lam_init):
    B, D, S = qt.shape
    tq, tk, tk_big = ATT_TQ, ATT_TK, ATT_TK_BIG
    assert tk == tq and tk_big == 2 * tk and S % tk_big == 0
    hd = 2 * DA_QK
    return pl.pallas_call(
        functools.partial(_attn_kernel, tq=tq, tk=tk, tk_big=tk_big, lam_init=lam_init),
        out_shape=jax.ShapeDtypeStruct((B, D, S), BF16),
        grid=(B, DA_HEADS, S // tq),
        in_specs=[
            pl.BlockSpec((1, hd, tq), lambda b, h, q: (b, h, q)),
            pl.BlockSpec((1, S, hd), lambda b, h, q: (b, 0, h)),
            pl.BlockSpec((1, DA_V, S), lambda b, h, q: (b, h, 0)),
            pl.BlockSpec((4, DA_QK), lambda b, h, q: (0, 0)),
            pl.BlockSpec((DA_V, 1), lambda b, h, q: (0, 0)),
        ],
        out_specs=pl.BlockSpec((1, DA_V, tq), lambda b, h, q: (b, h, q)),
        compiler_params=_params(("arbitrary", "arbitrary", "arbitrary")),
        name="diff_attention",
    )(qt, k, vt, lam_vecs, g_sub.reshape(DA_V, 1))


def _oproj_kernel(ot_ref, wt_ref, x_ref, mod_ref, o_ref):
    yt = jnp.dot(wt_ref[...], ot_ref[0], preferred_element_type=F32)
    g1 = mod_ref[0, 2:3, :]
    o_ref[0] = x_ref[0] + g1 * yt.T


def _oproj(ot, w_out, x, mod):
    B, S, D = x.shape
    ts = QKV_TS
    return pl.pallas_call(
        _oproj_kernel,
        out_shape=jax.ShapeDtypeStruct((B, S, D), F32),
        grid=(B, S // ts),
        in_specs=[
            pl.BlockSpec((1, D, ts), lambda b, s: (b, 0, s)),
            pl.BlockSpec((D, D), lambda b, s: (0, 0)),
            pl.BlockSpec((1, ts, D), lambda b, s: (b, s, 0)),
            pl.BlockSpec((1, 6, D), lambda b, s: (b, 0, 0)),
        ],
        out_specs=pl.BlockSpec((1, ts, D), lambda b, s: (b, s, 0)),
        compiler_params=_params(("arbitrary", "arbitrary")),
        name="attn_out_proj",
    )(ot, w_out.T.astype(BF16), x, mod)


def _top_values(s, k):
    vals = []
    for _ in range(k):
        m = jnp.max(s, axis=0, keepdims=True)
        vals.append(m)
        s = jnp.where(s == m, NEG_INF, s)
    return vals


def _top_values_128(s, k):
    n = s.shape[0] // 8
    assert n == 16 and k <= n
    blocks = [s[8 * v:8 * v + 8, :] for v in range(n)]
    size = 2
    while size <= n:
        stride = size // 2
        while stride >= 1:
            for i in range(n):
                j = i ^ stride
                if j > i:
                    hi = jnp.maximum(blocks[i], blocks[j])
                    lo = jnp.minimum(blocks[i], blocks[j])
                    blocks[i], blocks[j] = (hi, lo) if (i & size) == 0 else (lo, hi)
            stride //= 2
        size *= 2
    vals = []
    for t in range(k):
        head = blocks[0]
        m = jnp.max(head, axis=0, keepdims=True)
        vals.append(m)
        hit = head == m
        for r in range(k - 1 - t):
            blocks[r] = jnp.where(hit, blocks[r + 1], blocks[r])
    return vals


def _route_kernel(x_ref, mod_ref, gn_ref, wq_ref, sk_ref, ht_ref, cnt_ref, e1_ref, r2_ref, e2_ref):
    x = x_ref[...]
    sh2 = mod_ref[0, 3:4, :]
    sc2 = mod_ref[0, 4:5, :]
    h = _norm_mod(x, gn_ref[...], sc2, sh2)
    ht_ref[...] = h.T.astype(BF16)
    q = jnp.dot(h.astype(BF16), wq_ref[...], preferred_element_type=F32)
    K = PEER_TOPK
    nk = PEER_NKEYS
    for hd in range(PEER_HEADS):
        rs = slice(hd * nk, (hd + 1) * nk)
        s12 = []
        for p in range(2):
            c0 = (hd * 2 + p) * nk
            qhp = q[:, c0:c0 + nk].astype(BF16)
            s12.append(lax.dot_general(sk_ref[hd * 2 + p], qhp, (((1,), (1,)), ((), ())),
                                       preferred_element_type=F32))
        s1, s2 = s12
        a = _top_values_128(s1, K)
        b = _top_values_128(s2, K)
        r2 = jnp.full(s2.shape, float(K), F32)
        for l in reversed(range(K)):
            r2 = jnp.where(s2 >= b[l], float(l), r2)
        b_lo = jnp.concatenate(b[:8], axis=0)
        b_hi = jnp.concatenate(b[8:], axis=0)
        a_hi = jnp.concatenate(a[8:], axis=0)
        cand = [a[k] + b_lo for k in range(8)] + [a[0] + b_hi, a_hi + b[0]]
        c = _top_values(jnp.concatenate(cand, axis=0), K)
        tau = c[K - 1]
        zsum = jnp.ones_like(c[0])
        for k in range(1, K):
            zsum = zsum + jnp.exp(c[k] - c[0])
        a_all = jnp.concatenate(a, axis=0)
        ck = jnp.zeros(a_all.shape, F32)
        for l in range(K):
            ck = ck + jnp.where((a_all + b[l]) >= tau, 1.0, 0.0)
        cnt = jnp.zeros(s1.shape, F32)
        for k in range(K):
            cnt = jnp.where(s1 == a[k], ck[k:k + 1, :], cnt)
        cnt_ref[rs, :] = cnt
        e1_ref[rs, :] = jnp.exp(s1 - a[0])
        r2_ref[rs, :] = r2.astype(BF16)
        e2_ref[rs, :] = (0.5 * jnp.exp(s2 - b[0]) / zsum).astype(BF16)


def _dense_kernel(ht_ref, u_ref, vt_ref, cnt_ref, e1_ref, r2_ref, e2_ref, x_ref, mod_ref, fn_ref,
                  o_ref, acc_scr, *wa_scrs, sub_size, chunk, n_steps, final_norm):
    ec = pl.program_id(1)
    nk = PEER_NKEYS
    per = sub_size // nk
    group = F32_SUBLANES
    T = ht_ref.shape[1]

    @pl.when(ec == 0)
    def _():
        acc_scr[...] = jnp.zeros(acc_scr.shape, F32)

    def gate_block(i_rel):
        w = jnp.zeros((nk, T), BF16)
        grp, ii = divmod(i_rel, group)
        for hd in range(PEER_HEADS):
            rs = slice(hd * nk, (hd + 1) * nk)
            base = pl.multiple_of(hd * nk + ec * (len(wa_scrs) * per) + grp * group, group)
            cnt_row = cnt_ref[pl.ds(base, group), :][ii:ii + 1, :]
            e1_row = e1_ref[pl.ds(base, group), :][ii:ii + 1, :]
            cb = jnp.broadcast_to(cnt_row, (nk, T)).astype(BF16)
            eb = jnp.broadcast_to(e1_row, (nk, T)).astype(BF16)
            sel = r2_ref[rs, :] < cb
            w = w + jnp.where(sel, e2_ref[rs, :], jnp.zeros((), BF16)) * eb
        return w

    total = None
    for sidx, wa_scr in enumerate(wa_scrs):
        for c in range(per // chunk):
            rows = slice(c * chunk * nk, (c + 1) * chunk * nk)
            urows = slice(sidx * sub_size + c * chunk * nk, sidx * sub_size + (c + 1) * chunk * nk)
            act = jnp.dot(u_ref[urows, :], ht_ref[...], preferred_element_type=F32)
            ab = act.astype(BF16)
            g = ab * (1.0 + lax.erf(ab * 0.7071067811865476))
            w = jnp.concatenate([gate_block(sidx * per + c * chunk + j) for j in range(chunk)],
                                axis=0)
            wa_scr[rows, :] = w * g
        y = jnp.dot(vt_ref[:, sidx * sub_size:(sidx + 1) * sub_size], wa_scr[...],
                    preferred_element_type=F32)
        total = y if total is None else total + y
    acc_scr[...] += total

    @pl.when(ec == n_steps - 1)
    def _():
        g2 = mod_ref[0, 5:6, :]
        y = x_ref[...] + g2 * acc_scr[...].T
        if final_norm:
            ms = jnp.mean(y * y, axis=-1, keepdims=True)
            y = y * lax.rsqrt(ms + EPS) * fn_ref[...]
        o_ref[...] = y


def _peer(x, mod, gn, w_q, sub_keys, u_bf, vt_bf, fn=None):
    B, S, D = x.shape
    N = B * S
    H, NK = PEER_HEADS, PEER_NKEYS
    E = u_bf.shape[0]
    tr = ROUTE_T
    x2 = x.reshape(N, D)
    wq = w_q.reshape(D, H * 2 * NK).astype(BF16)
    sk = sub_keys.reshape(H * 2, NK, NK).astype(BF16)
    ht, cnt, e1, r2, e2 = pl.pallas_call(
        _route_kernel,
        out_shape=(
            jax.ShapeDtypeStruct((D, N), BF16),
            jax.ShapeDtypeStruct((H * NK, N), F32),
            jax.ShapeDtypeStruct((H * NK, N), F32),
            jax.ShapeDtypeStruct((H * NK, N), BF16),
            jax.ShapeDtypeStruct((H * NK, N), BF16),
        ),
        grid=(N // tr,),
        in_specs=[
            pl.BlockSpec((tr, D), lambda i: (i, 0)),
            pl.BlockSpec((1, 6, D), lambda i: ((i * tr) // S, 0, 0)),
            pl.BlockSpec((1, D), lambda i: (0, 0)),
            pl.BlockSpec((D, H * 2 * NK), lambda i: (0, 0)),
            pl.BlockSpec((H * 2, NK, NK), lambda i: (0, 0, 0)),
        ],
        out_specs=(
            pl.BlockSpec((D, tr), lambda i: (0, i)),
            pl.BlockSpec((H * NK, tr), lambda i: (0, i)),
            pl.BlockSpec((H * NK, tr), lambda i: (0, i)),
            pl.BlockSpec((H * NK, tr), lambda i: (0, i)),
            pl.BlockSpec((H * NK, tr), lambda i: (0, i)),
        ),
        compiler_params=_params(("arbitrary",)),
        name="peer_route",
    )(x2, mod, gn.reshape(1, D), wq, sk)

    td, sub, nsub = DENSE_T, DENSE_SUB, DENSE_NSUB
    ec = sub * nsub
    assert (sub // NK) % 8 == 0 and (sub // NK) % DENSE_CHUNK == 0 and E % ec == 0
    assert S % td == 0
    rank_spec = pl.BlockSpec((H * NK, td), lambda t, e: (0, t))
    fn_arr = jnp.ones((1, D), F32) if fn is None else fn.reshape(1, D)
    out = pl.pallas_call(
        functools.partial(_dense_kernel, sub_size=sub, chunk=DENSE_CHUNK, n_steps=E // ec,
                          final_norm=fn is not None),
        out_shape=jax.ShapeDtypeStruct((N, D), F32),
        grid=(N // td, E // ec),
        in_specs=[
            pl.BlockSpec((D, td), lambda t, e: (0, t)),
            pl.BlockSpec((ec, D), lambda t, e: (e, 0)),
            pl.BlockSpec((D, ec), lambda t, e: (0, e)),
            rank_spec, rank_spec, rank_spec, rank_spec,
            pl.BlockSpec((td, D), lambda t, e: (t, 0)),
            pl.BlockSpec((1, 6, D), lambda t, e: ((t * td) // S, 0, 0)),
            pl.BlockSpec((1, D), lambda t, e: (0, 0)),
        ],
        out_specs=pl.BlockSpec((td, D), lambda t, e: (t, 0)),
        scratch_shapes=[pltpu.VMEM((D, td), F32)] + [pltpu.VMEM((sub, td), BF16)] * nsub,
        compiler_params=_params(("arbitrary", "arbitrary")),
        name="peer_experts",
    )(ht, u_bf, vt_bf, cnt, e1, r2, e2, x2, mod, fn_arr)
    return out.reshape(B, S, D)


def kernel(x, c, positions, ada_w, ada_b, norm_mix, norm_ffn, ev_w_in, ev_g_v, ev_w_s, ev_b_s,
           ev_w_pool, ev_pool_scale, ev_w_out, od_w_in, od_lam_q1, od_lam_k1, od_lam_q2,
           od_lam_k2, od_g_sub, od_w_out, peer_w_q, peer_sub_keys, peer_u, peer_v, final_norm):
    B, S, D = x.shape
    depth = ada_w.shape[0]
    inv_freq = 1.0 / (ROPE_THETA ** (jnp.arange(0, DA_QK, 2, dtype=F32) / DA_QK))
    ang = positions.astype(F32)[..., None] * inv_freq
    cos_t = jnp.swapaxes(jnp.cos(ang), 1, 2)
    sin_t = jnp.swapaxes(jnp.sin(ang), 1, 2)
    mods = _ada(c, ada_w, ada_b)
    for l in range(depth):
        mod = mods[l]
        if l % 2 == 0:
            e = l // 2
            x = _even_layer(x, mod, norm_mix[l], ev_w_in[e], ev_g_v[e], ev_w_s[e], ev_b_s[e],
                            ev_w_pool[e], ev_pool_scale[e], ev_w_out[e])
        else:
            o = l // 2
            lam_init = 0.8 - 0.6 * math.exp(-0.3 * l)
            qt, k, vt = _qkv(x, mod, norm_mix[l], od_w_in[o], cos_t, sin_t)
            lam_vecs = jnp.stack([od_lam_q1[o], od_lam_k1[o], od_lam_q2[o], od_lam_k2[o]])
            ot = _attention(qt, k, vt, lam_vecs, od_g_sub[o], lam_init)
            x = _oproj(ot, od_w_out[o], x, mod)
        x = _peer(x, mod, norm_ffn[l], peer_w_q[l], peer_sub_keys[l],
                  peer_u[l].astype(BF16), peer_v[l].T.astype(BF16),
                  fn=final_norm if l == depth - 1 else None)
    return x
```

```python
import functools
import math

import jax
import jax.numpy as jnp
from jax import lax
from jax.experimental import pallas as pl
from jax.experimental.pallas import tpu as pltpu

F32 = jnp.float32
BF16 = jnp.bfloat16
EPS = 1e-6
NEG_INF = float("-inf")

GM_GROUPS = 4
GM_CH = 128
GM_WIDTH = GM_GROUPS * GM_CH
GM_CHUNK = 128
POOL_WINDOWS = (2, 4, 8, 16)
POOL_CH = 128
POOL_WIDTH = len(POOL_WINDOWS) * POOL_CH
POOL_HALO = max(POOL_WINDOWS)
DA_HEADS = 8
DA_QK = 64
DA_V = 128
ROPE_THETA = 10000.0
PEER_HEADS = 8
PEER_NKEYS = 128
PEER_TOPK = 16

F32_SUBLANES = 8

V7X_VMEM_LIMIT = 56 * 1024 * 1024

EVEN_TS = 512
QKV_TS = 512
ATT_TQ = 512
ATT_TK = 512
ATT_TK_BIG = 1024
ROUTE_T = 256
DENSE_T = 512
DENSE_SUB = 1024
DENSE_NSUB = 2
DENSE_CHUNK = 2


def _params(sem):
    return pltpu.CompilerParams(dimension_semantics=sem, vmem_limit_bytes=V7X_VMEM_LIMIT)


def _gelu(x):
    return 0.5 * x * (1.0 + lax.erf(x * 0.7071067811865476))


def _norm_mod(x, gn, sc, sh):
    ms = jnp.mean(x * x, axis=-1, keepdims=True)
    return x * lax.rsqrt(ms + EPS) * gn * (1.0 + sc) + sh


def _ada_kernel(c_ref, w_ref, b_ref, o_ref):
    c = c_ref[...]
    ca = c * jax.nn.sigmoid(c)
    o_ref[0, 0] = jnp.dot(ca, w_ref[0], preferred_element_type=F32,
                          precision=lax.Precision.HIGHEST) + b_ref[0, 0]


def _ada(c, ada_w, ada_b):
    L, D, _ = ada_w.shape
    B = c.shape[0]
    out = pl.pallas_call(
        _ada_kernel,
        out_shape=jax.ShapeDtypeStruct((L, 6, B, D), F32),
        grid=(L, 6),
        in_specs=[
            pl.BlockSpec((B, D), lambda l, j: (0, 0)),
            pl.BlockSpec((1, D, D), lambda l, j: (l, 0, j)),
            pl.BlockSpec((1, 1, 1, D), lambda l, j: (l, j, 0, 0)),
        ],
        out_specs=pl.BlockSpec((1, 1, B, D), lambda l, j: (l, j, 0, 0)),
        compiler_params=_params(("arbitrary", "arbitrary")),
        name="ada",
    )(c, ada_w, ada_b.reshape(L, 6, 1, D))
    return jnp.swapaxes(out, 1, 2)


def _even_kernel(x_ref, mod_ref, gn_ref, win_ref, gv_ref, ws_ref, bs_ref, wp_ref, ls_ref,
                 wout_ref, o_ref, ext_ref, mix_ref, *, ts):
    si = pl.program_id(1)
    x = x_ref[0]
    sh1 = mod_ref[0, 0:1, :]
    sc1 = mod_ref[0, 1:2, :]
    g1 = mod_ref[0, 2:3, :]
    h = _norm_mod(x, gn_ref[...], sc1, sh1)
    proj = jnp.dot(h.astype(BF16), win_ref[...], preferred_element_type=F32)

    z = _gelu(proj[:, :2 * GM_WIDTH])
    u = z[:, :GM_WIDTH]
    v = z[:, GM_WIDTH:]
    mu = jnp.mean(v, axis=-1, keepdims=True)
    d = v - mu
    var = jnp.mean(d * d, axis=-1, keepdims=True)
    vn = (d * lax.rsqrt(var + EPS) * gv_ref[...]).astype(BF16)
    r = lax.broadcasted_iota(jnp.int32, (GM_CHUNK, GM_CHUNK), 0)
    cidx = lax.broadcasted_iota(jnp.int32, (GM_CHUNK, GM_CHUNK), 1)
    causal = r >= cidx
    for g in range(GM_GROUPS):
        wg = jnp.where(causal, ws_ref[g], 0.0).astype(BF16)
        for c in range(ts // GM_CHUNK):
            rs = slice(c * GM_CHUNK, (c + 1) * GM_CHUNK)
            cs = slice(g * GM_CH, (g + 1) * GM_CH)
            sv = jnp.dot(wg, vn[rs, cs], preferred_element_type=F32) + bs_ref[g]
            mix_ref[rs, cs] = (u[rs, cs] * sv).astype(BF16)

    p = proj[:, 2 * GM_WIDTH:]

    @pl.when(si == 0)
    def _():
        ext_ref[0:POOL_HALO, :] = jnp.zeros((POOL_HALO, POOL_WIDTH), F32)

    ext_ref[POOL_HALO:POOL_HALO + ts, :] = p
    t = si * ts + lax.broadcasted_iota(jnp.int32, (ts, POOL_CH), 0)
    for g, w in enumerate(POOL_WINDOWS):
        cs = slice(g * POOL_CH, (g + 1) * POOL_CH)
        acc = ext_ref[POOL_HALO:POOL_HALO + ts, cs]
        for k in range(1, w):
            acc = acc + ext_ref[POOL_HALO - k:POOL_HALO - k + ts, cs]
        cnt = jnp.minimum(t + 1, w).astype(F32)
        pooled = acc / cnt - p[:, cs]
        yb = jnp.dot(pooled.astype(BF16), wp_ref[g], preferred_element_type=F32) * ls_ref[:, cs]
        mix_ref[:, GM_WIDTH + g * POOL_CH:GM_WIDTH + (g + 1) * POOL_CH] = yb.astype(BF16)
    ext_ref[0:POOL_HALO, :] = ext_ref[ts:ts + POOL_HALO, :]

    y = jnp.dot(mix_ref[...], wout_ref[...], preferred_element_type=F32)
    o_ref[0] = x + g1 * y


def _even_layer(x, mod, gn, w_in, g_v, w_s, b_s, w_pool, ls, w_out):
    B, S, D = x.shape
    ts = EVEN_TS
    n_in = w_in.shape[1]
    bsb = jnp.broadcast_to(b_s[:, :, None], (GM_GROUPS, GM_CHUNK, GM_CH))
    return pl.pallas_call(
        functools.partial(_even_kernel, ts=ts),
        out_shape=jax.ShapeDtypeStruct((B, S, D), F32),
        grid=(B, S // ts),
        in_specs=[
            pl.BlockSpec((1, ts, D), lambda b, s: (b, s, 0)),
            pl.BlockSpec((1, 6, D), lambda b, s: (b, 0, 0)),
            pl.BlockSpec((1, D), lambda b, s: (0, 0)),
            pl.BlockSpec((D, n_in), lambda b, s: (0, 0)),
            pl.BlockSpec((1, GM_WIDTH), lambda b, s: (0, 0)),
            pl.BlockSpec((GM_GROUPS, GM_CHUNK, GM_CHUNK), lambda b, s: (0, 0, 0)),
            pl.BlockSpec((GM_GROUPS, GM_CHUNK, GM_CH), lambda b, s: (0, 0, 0)),
            pl.BlockSpec((len(POOL_WINDOWS), POOL_CH, POOL_CH), lambda b, s: (0, 0, 0)),
            pl.BlockSpec((1, POOL_WIDTH), lambda b, s: (0, 0)),
            pl.BlockSpec((GM_WIDTH + POOL_WIDTH, D), lambda b, s: (0, 0)),
        ],
        out_specs=pl.BlockSpec((1, ts, D), lambda b, s: (b, s, 0)),
        scratch_shapes=[
            pltpu.VMEM((POOL_HALO + ts, POOL_WIDTH), F32),
            pltpu.VMEM((ts, GM_WIDTH + POOL_WIDTH), BF16),
        ],
        compiler_params=_params(("arbitrary", "arbitrary")),
        name="even_mixer",
    )(x, mod, gn.reshape(1, D), w_in.astype(BF16), g_v.reshape(1, -1), w_s, bsb,
      w_pool.astype(BF16), ls.reshape(1, -1), w_out.astype(BF16))


def _qkv_kernel(x_ref, mod_ref, gn_ref, wt_ref, cos_ref, sin_ref, qt_ref, k_ref, vt_ref,
                kt_scr, *, d_model):
    x = x_ref[0]
    sh1 = mod_ref[0, 0:1, :]
    sc1 = mod_ref[0, 1:2, :]
    h = _norm_mod(x, gn_ref[...], sc1, sh1)
    ht = h.T.astype(BF16)
    pt = jnp.dot(wt_ref[...], ht, preferred_element_type=F32)
    cos = cos_ref[0]
    sin = sin_ref[0]
    half = DA_QK // 2
    scale = DA_QK ** -0.5 * math.log2(math.e)
    for g in range(2 * DA_HEADS):
        r0 = g * DA_QK
        t1 = pt[r0:r0 + half]
        t2 = pt[r0 + half:r0 + DA_QK]
        qt_ref[0, r0:r0 + half, :] = ((t1 * cos - t2 * sin) * scale).astype(BF16)
        qt_ref[0, r0 + half:r0 + DA_QK, :] = ((t2 * cos + t1 * sin) * scale).astype(BF16)
        k0 = d_model + r0
        t1 = pt[k0:k0 + half]
        t2 = pt[k0 + half:k0 + DA_QK]
        kt_scr[r0:r0 + half, :] = t1 * cos - t2 * sin
        kt_scr[r0 + half:r0 + DA_QK, :] = t2 * cos + t1 * sin
    k_ref[0] = kt_scr[...].T.astype(BF16)
    vt_ref[0] = pt[2 * d_model:3 * d_model].astype(BF16)


def _qkv(x, mod, gn, w_in, cos_t, sin_t):
    B, S, D = x.shape
    ts = QKV_TS
    wt = w_in.T.astype(BF16)
    return pl.pallas_call(
        functools.partial(_qkv_kernel, d_model=D),
        out_shape=(
            jax.ShapeDtypeStruct((B, D, S), BF16),
            jax.ShapeDtypeStruct((B, S, D), BF16),
            jax.ShapeDtypeStruct((B, D, S), BF16),
        ),
        grid=(B, S // ts),
        in_specs=[
            pl.BlockSpec((1, ts, D), lambda b, s: (b, s, 0)),
            pl.BlockSpec((1, 6, D), lambda b, s: (b, 0, 0)),
            pl.BlockSpec((1, D), lambda b, s: (0, 0)),
            pl.BlockSpec((3 * D, D), lambda b, s: (0, 0)),
            pl.BlockSpec((1, DA_QK // 2, ts), lambda b, s: (b, 0, s)),
            pl.BlockSpec((1, DA_QK // 2, ts), lambda b, s: (b, 0, s)),
        ],
        out_specs=(
            pl.BlockSpec((1, D, ts), lambda b, s: (b, 0, s)),
            pl.BlockSpec((1, ts, D), lambda b, s: (b, s, 0)),
            pl.BlockSpec((1, D, ts), lambda b, s: (b, 0, s)),
        ),
        scratch_shapes=[pltpu.VMEM((D, ts), F32)],
        compiler_params=_params(("arbitrary", "arbitrary")),
        name="qkv_rope",
    )(x, mod, gn.reshape(1, D), wt, cos_t, sin_t)


def _attn_kernel(qt_ref, k_ref, vt_ref, lam_ref, gs_ref, o_ref, *, tq, tk, tk_big, lam_init):
    qi = pl.program_id(2)
    qt = qt_ref[0]
    row = lax.broadcasted_iota(jnp.int32, qt.shape, 0)
    zero = jnp.zeros_like(qt)
    q12 = jnp.concatenate([jnp.where(row < DA_QK, qt, zero), jnp.where(row >= DA_QK, qt, zero)],
                          axis=1)

    def update(start, size, masked, carry):
        m_old, l_old, acc_old = carry
        off = pl.multiple_of(start, tk)
        kb = k_ref[0, pl.ds(off, size), :]
        vb = vt_ref[0, :, pl.ds(off, size)]
        s = jnp.dot(kb, q12, preferred_element_type=F32)
        if masked:
            kpos = off + lax.broadcasted_iota(jnp.int32, s.shape, 0)
            lane = lax.broadcasted_iota(jnp.int32, s.shape, 1)
            qpos = qi * tq + jnp.where(lane >= tq, lane - tq, lane)
            s = jnp.where(kpos <= qpos, s, NEG_INF)
        m_new = jnp.maximum(m_old, jnp.max(s, axis=0, keepdims=True))
        alpha = jnp.exp2(m_old - m_new)
        p = jnp.exp2(s - m_new)
        l_new = alpha * l_old + jnp.sum(p, axis=0, keepdims=True)
        acc_new = alpha * acc_old + jnp.dot(vb, p.astype(BF16), preferred_element_type=F32)
        return m_new, l_new, acc_new

    init = (jnp.full((1, 2 * tq), NEG_INF, F32), jnp.zeros((1, 2 * tq), F32),
            jnp.zeros((DA_V, 2 * tq), F32))
    visible = qi * tq
    n_big = visible // tk_big
    carry = lax.fori_loop(0, n_big, lambda j, c: update(j * tk_big, tk_big, False, c), init)
    carry = lax.cond(visible - n_big * tk_big >= tk,
                     lambda c: update(n_big * tk_big, tk, False, c), lambda c: c, carry)
    _, l, acc = update(visible, tk, True, carry)

    lv = lam_ref[...]
    la = jnp.sum(lv[0:1] * lv[1:2], axis=-1, keepdims=True)
    lb = jnp.sum(lv[2:3] * lv[3:4], axis=-1, keepdims=True)
    lam = jnp.exp(la) - jnp.exp(lb) + lam_init
    o = acc[:, :tq] / l[:, :tq] - lam * (acc[:, tq:] / l[:, tq:])
    ms = jnp.mean(o * o, axis=0, keepdims=True)
    o = o * lax.rsqrt(ms + EPS) * gs_ref[...] * (1.0 - lam_init)
    o_ref[0] = o.astype(BF16)


def _attention(qt, k, vt, lam_vecs, g_sub, lam_init):
    B, D, S = qt.shape
    tq, tk, tk_big = ATT_TQ, ATT_TK, ATT_TK_BIG
    assert tk == tq and tk_big == 2 * tk and S % tk == 0
    hd = 2 * DA_QK
    return pl.pallas_call(
        functools.partial(_attn_kernel, tq=tq, tk=tk, tk_big=tk_big, lam_init=lam_init),
        out_shape=jax.ShapeDtypeStruct((B, D, S), BF16),
        grid=(B, DA_HEADS, S // tq),
        in_specs=[
            pl.BlockSpec((1, hd, tq), lambda b, h, q: (b, h, q)),
            pl.BlockSpec((1, S, hd), lambda b, h, q: (b, 0, h)),
            pl.BlockSpec((1, DA_V, S), lambda b, h, q: (b, h, 0)),
            pl.BlockSpec((4, DA_QK), lambda b, h, q: (0, 0)),
            pl.BlockSpec((DA_V, 1), lambda b, h, q: (0, 0)),
        ],
        out_specs=pl.BlockSpec((1, DA_V, tq), lambda b, h, q: (b, h, q)),
        compiler_params=_params(("arbitrary", "arbitrary", "arbitrary")),
        name="diff_attention",
    )(qt, k, vt, lam_vecs, g_sub.reshape(DA_V, 1))


def _oproj_kernel(ot_ref, wt_ref, x_ref, mod_ref, o_ref):
    yt = jnp.dot(wt_ref[...], ot_ref[0], preferred_element_type=F32)
    g1 = mod_ref[0, 2:3, :]
    o_ref[0] = x_ref[0] + g1 * yt.T


def _oproj(ot, w_out, x, mod):
    B, S, D = x.shape
    ts = QKV_TS
    return pl.pallas_call(
        _oproj_kernel,
        out_shape=jax.ShapeDtypeStruct((B, S, D), F32),
        grid=(B, S // ts),
        in_specs=[
            pl.BlockSpec((1, D, ts), lambda b, s: (b, 0, s)),
            pl.BlockSpec((D, D), lambda b, s: (0, 0)),
            pl.BlockSpec((1, ts, D), lambda b, s: (b, s, 0)),
            pl.BlockSpec((1, 6, D), lambda b, s: (b, 0, 0)),
        ],
        out_specs=pl.BlockSpec((1, ts, D), lambda b, s: (b, s, 0)),
        compiler_params=_params(("arbitrary", "arbitrary")),
        name="attn_out_proj",
    )(ot, w_out.T.astype(BF16), x, mod)


def _top_pair_sums(a, b, k):
    n = F32_SUBLANES
    assert k == 2 * n and len(a) == k and len(b) == k
    b_lo = jnp.concatenate(b[:n], axis=0)
    lists = [a[p] + b_lo for p in range(n)]
    row0 = a[0] + jnp.concatenate(b[n:], axis=0)
    col0 = jnp.concatenate(a[n:], axis=0) + b[0]
    vals = []
    for _ in range(k):
        head = jnp.maximum(jnp.maximum(lists[0], row0), col0)
        m = jnp.max(head, axis=0, keepdims=True)
        vals.append(m)
        hit = lists[0] == m
        for r in range(n - 1):
            lists[r] = jnp.where(hit, lists[r + 1], lists[r])
        lists[n - 1] = jnp.where(hit, NEG_INF, lists[n - 1])
        row0 = jnp.where(row0 == m, NEG_INF, row0)
        col0 = jnp.where(col0 == m, NEG_INF, col0)
    return vals


def _rank_among(s, b):
    assert len(b) == 16
    lo8 = s >= b[7]
    lo4 = s >= jnp.where(lo8, b[3], b[11])
    lo2 = s >= jnp.where(lo8, jnp.where(lo4, b[1], b[5]), jnp.where(lo4, b[9], b[13]))
    lo1 = s >= jnp.where(
        lo8,
        jnp.where(lo4, jnp.where(lo2, b[0], b[2]), jnp.where(lo2, b[4], b[6])),
        jnp.where(lo4, jnp.where(lo2, b[8], b[10]), jnp.where(lo2, b[12], b[14])))
    rank = (jnp.where(lo8, 0.0, 8.0) + jnp.where(lo4, 0.0, 4.0)
            + jnp.where(lo2, 0.0, 2.0) + jnp.where(lo1, 0.0, 1.0))
    return jnp.where(s >= b[15], rank, 16.0)


def _top_values_128(s, k):
    n = s.shape[0] // 8
    assert n == 16 and k <= n
    blocks = [s[8 * v:8 * v + 8, :] for v in range(n)]
    size = 2
    while size <= n:
        stride = size // 2
        while stride >= 1:
            for i in range(n):
                j = i ^ stride
                if j > i:
                    hi = jnp.maximum(blocks[i], blocks[j])
                    lo = jnp.minimum(blocks[i], blocks[j])
                    blocks[i], blocks[j] = (hi, lo) if (i & size) == 0 else (lo, hi)
            stride //= 2
        size *= 2
    vals = []
    for t in range(k):
        head = blocks[0]
        m = jnp.max(head, axis=0, keepdims=True)
        vals.append(m)
        hit = head == m
        for r in range(k - 1 - t):
            blocks[r] = jnp.where(hit, blocks[r + 1], blocks[r])
    return vals


def _route_kernel(x_ref, mod_ref, gn_ref, wq_ref, sk_ref, ht_ref, cnt_ref, e1_ref, r2_ref, e2_ref):
    x = x_ref[...]
    sh2 = mod_ref[0, 3:4, :]
    sc2 = mod_ref[0, 4:5, :]
    h = _norm_mod(x, gn_ref[...], sc2, sh2)
    ht_ref[...] = h.T.astype(BF16)
    q = jnp.dot(h.astype(BF16), wq_ref[...], preferred_element_type=F32)
    K = PEER_TOPK
    nk = PEER_NKEYS
    for hd in range(PEER_HEADS):
        rs = slice(hd * nk, (hd + 1) * nk)
        s12 = []
        for p in range(2):
            c0 = (hd * 2 + p) * nk
            qhp = q[:, c0:c0 + nk].astype(BF16)
            s12.append(lax.dot_general(sk_ref[hd * 2 + p], qhp, (((1,), (1,)), ((), ())),
                                       preferred_element_type=F32))
        s1, s2 = s12
        a = _top_values_128(s1, K)
        b = _top_values_128(s2, K)
        r2 = _rank_among(s2, b)
        c = _top_pair_sums(a, b, K)
        tau = c[K - 1]
        zsum = jnp.ones_like(c[0])
        for k in range(1, K):
            zsum = zsum + jnp.exp(c[k] - c[0])
        a_all = jnp.concatenate(a, axis=0)
        ck = jnp.zeros(a_all.shape, F32)
        for l in range(K):
            ck = ck + jnp.where((a_all + b[l]) >= tau, 1.0, 0.0)
        cnt = jnp.zeros(s1.shape, F32)
        for k in range(K):
            cnt = jnp.where(s1 == a[k], ck[k:k + 1, :], cnt)
        cnt_ref[rs, :] = cnt
        e1_ref[rs, :] = jnp.exp(s1 - a[0])
        r2_ref[rs, :] = r2.astype(BF16)
        e2_ref[rs, :] = (0.5 * jnp.exp(s2 - b[0]) / zsum).astype(BF16)


def _dense_kernel(ht_ref, u_ref, vt_ref, cnt_ref, e1_ref, r2_ref, e2_ref, x_ref, mod_ref, fn_ref,
                  o_ref, acc_scr, *wa_scrs, sub_size, chunk, n_steps, final_norm):
    ec = pl.program_id(1)
    nk = PEER_NKEYS
    per = sub_size // nk
    group = F32_SUBLANES
    T = ht_ref.shape[1]

    @pl.when(ec == 0)
    def _():
        acc_scr[...] = jnp.zeros(acc_scr.shape, F32)

    def gate_block(i_rel):
        w = jnp.zeros((nk, T), BF16)
        grp, ii = divmod(i_rel, group)
        for hd in range(PEER_HEADS):
            rs = slice(hd * nk, (hd + 1) * nk)
            base = pl.multiple_of(hd * nk + ec * (len(wa_scrs) * per) + grp * group, group)
            cnt_row = cnt_ref[pl.ds(base, group), :][ii:ii + 1, :]
            e1_row = e1_ref[pl.ds(base, group), :][ii:ii + 1, :]
            cb = jnp.broadcast_to(cnt_row, (nk, T)).astype(BF16)
            eb = jnp.broadcast_to(e1_row, (nk, T)).astype(BF16)
            sel = r2_ref[rs, :] < cb
            w = w + jnp.where(sel, e2_ref[rs, :], jnp.zeros((), BF16)) * eb
        return w

    total = None
    for sidx, wa_scr in enumerate(wa_scrs):
        for c in range(per // chunk):
            rows = slice(c * chunk * nk, (c + 1) * chunk * nk)
            urows = slice(sidx * sub_size + c * chunk * nk, sidx * sub_size + (c + 1) * chunk * nk)
            act = jnp.dot(u_ref[urows, :], ht_ref[...], preferred_element_type=F32)
            ab = act.astype(BF16)
            g = ab * (1.0 + lax.erf(ab * 0.7071067811865476))
            w = jnp.concatenate([gate_block(sidx * per + c * chunk + j) for j in range(chunk)],
                                axis=0)
            wa_scr[rows, :] = w * g
        y = jnp.dot(vt_ref[:, sidx * sub_size:(sidx + 1) * sub_size], wa_scr[...],
                    preferred_element_type=F32)
        total = y if total is None else total + y
    acc_scr[...] += total

    @pl.when(ec == n_steps - 1)
    def _():
        g2 = mod_ref[0, 5:6, :]
        y = x_ref[...] + g2 * acc_scr[...].T
        if final_norm:
            ms = jnp.mean(y * y, axis=-1, keepdims=True)
            y = y * lax.rsqrt(ms + EPS) * fn_ref[...]
        o_ref[...] = y


def _peer(x, mod, gn, w_q, sub_keys, u_bf, vt_bf, fn=None):
    B, S, D = x.shape
    N = B * S
    H, NK = PEER_HEADS, PEER_NKEYS
    E = u_bf.shape[0]
    tr = ROUTE_T
    x2 = x.reshape(N, D)
    wq = w_q.reshape(D, H * 2 * NK).astype(BF16)
    sk = sub_keys.reshape(H * 2, NK, NK).astype(BF16)
    ht, cnt, e1, r2, e2 = pl.pallas_call(
        _route_kernel,
        out_shape=(
            jax.ShapeDtypeStruct((D, N), BF16),
            jax.ShapeDtypeStruct((H * NK, N), F32),
            jax.ShapeDtypeStruct((H * NK, N), F32),
            jax.ShapeDtypeStruct((H * NK, N), BF16),
            jax.ShapeDtypeStruct((H * NK, N), BF16),
        ),
        grid=(N // tr,),
        in_specs=[
            pl.BlockSpec((tr, D), lambda i: (i, 0)),
            pl.BlockSpec((1, 6, D), lambda i: ((i * tr) // S, 0, 0)),
            pl.BlockSpec((1, D), lambda i: (0, 0)),
            pl.BlockSpec((D, H * 2 * NK), lambda i: (0, 0)),
            pl.BlockSpec((H * 2, NK, NK), lambda i: (0, 0, 0)),
        ],
        out_specs=(
            pl.BlockSpec((D, tr), lambda i: (0, i)),
            pl.BlockSpec((H * NK, tr), lambda i: (0, i)),
            pl.BlockSpec((H * NK, tr), lambda i: (0, i)),
            pl.BlockSpec((H * NK, tr), lambda i: (0, i)),
            pl.BlockSpec((H * NK, tr), lambda i: (0, i)),
        ),
        compiler_params=_params(("arbitrary",)),
        name="peer_route",
    )(x2, mod, gn.reshape(1, D), wq, sk)

    td, sub, nsub = DENSE_T, DENSE_SUB, DENSE_NSUB
    ec = sub * nsub
    assert (sub // NK) % 8 == 0 and (sub // NK) % DENSE_CHUNK == 0 and E % ec == 0
    assert S % td == 0
    rank_spec = pl.BlockSpec((H * NK, td), lambda t, e: (0, t))
    fn_arr = jnp.ones((1, D), F32) if fn is None else fn.reshape(1, D)
    out = pl.pallas_call(
        functools.partial(_dense_kernel, sub_size=sub, chunk=DENSE_CHUNK, n_steps=E // ec,
                          final_norm=fn is not None),
        out_shape=jax.ShapeDtypeStruct((N, D), F32),
        grid=(N // td, E // ec),
        in_specs=[
            pl.BlockSpec((D, td), lambda t, e: (0, t)),
            pl.BlockSpec((ec, D), lambda t, e: (e, 0)),
            pl.BlockSpec((D, ec), lambda t, e: (0, e)),
            rank_spec, rank_spec, rank_spec, rank_spec,
            pl.BlockSpec((td, D), lambda t, e: (t, 0)),
            pl.BlockSpec((1, 6, D), lambda t, e: ((t * td) // S, 0, 0)),
            pl.BlockSpec((1, D), lambda t, e: (0, 0)),
        ],
        out_specs=pl.BlockSpec((td, D), lambda t, e: (t, 0)),
        scratch_shapes=[pltpu.VMEM((D, td), F32)] + [pltpu.VMEM((sub, td), BF16)] * nsub,
        compiler_params=_params(("arbitrary", "arbitrary")),
        name="peer_experts",
    )(ht, u_bf, vt_bf, cnt, e1, r2, e2, x2, mod, fn_arr)
    return out.reshape(B, S, D)


def kernel(x, c, positions, ada_w, ada_b, norm_mix, norm_ffn, ev_w_in, ev_g_v, ev_w_s, ev_b_s,
           ev_w_pool, ev_pool_scale, ev_w_out, od_w_in, od_lam_q1, od_lam_k1, od_lam_q2,
           od_lam_k2, od_g_sub, od_w_out, peer_w_q, peer_sub_keys, peer_u, peer_v, final_norm):
    B, S, D = x.shape
    depth = ada_w.shape[0]
    inv_freq = 1.0 / (ROPE_THETA ** (jnp.arange(0, DA_QK, 2, dtype=F32) / DA_QK))
    ang = positions.astype(F32)[..., None] * inv_freq
    cos_t = jnp.swapaxes(jnp.cos(ang), 1, 2)
    sin_t = jnp.swapaxes(jnp.sin(ang), 1, 2)
    mods = _ada(c, ada_w, ada_b)
    for l in range(depth):
        mod = mods[l]
        if l % 2 == 0:
            e = l // 2
            x = _even_layer(x, mod, norm_mix[l], ev_w_in[e], ev_g_v[e], ev_w_s[e], ev_b_s[e],
                            ev_w_pool[e], ev_pool_scale[e], ev_w_out[e])
        else:
            o = l // 2
            lam_init = 0.8 - 0.6 * math.exp(-0.3 * l)
            qt, k, vt = _qkv(x, mod, norm_mix[l], od_w_in[o], cos_t, sin_t)
            lam_vecs = jnp.stack([od_lam_q1[o], od_lam_k1[o], od_lam_q2[o], od_lam_k2[o]])
            ot = _attention(qt, k, vt, lam_vecs, od_g_sub[o], lam_init)
            x = _oproj(ot, od_w_out[o], x, mod)
        x = _peer(x, mod, norm_ffn[l], peer_w_q[l], peer_sub_keys[l],
                  peer_u[l].astype(BF16), peer_v[l].T.astype(BF16),
                  fn=final_norm if l == depth - 1 else None)
    return x
```

```python
import functools
import math

import jax
import jax.numpy as jnp
from jax import lax
from jax.experimental import pallas as pl
from jax.experimental.pallas import tpu as pltpu

F32 = jnp.float32
BF16 = jnp.bfloat16
EPS = 1e-6
NEG_INF = float("-inf")

GM_GROUPS = 4
GM_CH = 128
GM_WIDTH = GM_GROUPS * GM_CH
GM_CHUNK = 128
POOL_WINDOWS = (2, 4, 8, 16)
POOL_CH = 128
POOL_WIDTH = len(POOL_WINDOWS) * POOL_CH
POOL_HALO = max(POOL_WINDOWS)
DA_HEADS = 8
DA_QK = 64
DA_V = 128
ROPE_THETA = 10000.0
PEER_HEADS = 8
PEER_NKEYS = 128
PEER_TOPK = 16

F32_SUBLANES = 8

V7X_VMEM_LIMIT = 56 * 1024 * 1024

EVEN_TS = 512
QKV_TS = 512
ATT_TQ = 512
ATT_TK = 512
ATT_TK_BIG = 1024
ATT_HEADS = 4
ROUTE_T = 256
DENSE_T = 512
DENSE_SUB = 1024
DENSE_NSUB = 2
DENSE_CHUNK = 2


def _params(sem):
    return pltpu.CompilerParams(dimension_semantics=sem, vmem_limit_bytes=V7X_VMEM_LIMIT)


def _gelu(x):
    return 0.5 * x * (1.0 + lax.erf(x * 0.7071067811865476))


def _norm_mod(x, gn, sc, sh):
    ms = jnp.mean(x * x, axis=-1, keepdims=True)
    return x * lax.rsqrt(ms + EPS) * gn * (1.0 + sc) + sh


def _ada_kernel(c_ref, w_ref, b_ref, o_ref):
    c = c_ref[...]
    ca = c * jax.nn.sigmoid(c)
    o_ref[0, 0] = jnp.dot(ca, w_ref[0], preferred_element_type=F32,
                          precision=lax.Precision.HIGHEST) + b_ref[0, 0]


def _ada(c, ada_w, ada_b):
    L, D, _ = ada_w.shape
    B = c.shape[0]
    out = pl.pallas_call(
        _ada_kernel,
        out_shape=jax.ShapeDtypeStruct((L, 6, B, D), F32),
        grid=(L, 6),
        in_specs=[
            pl.BlockSpec((B, D), lambda l, j: (0, 0)),
            pl.BlockSpec((1, D, D), lambda l, j: (l, 0, j)),
            pl.BlockSpec((1, 1, 1, D), lambda l, j: (l, j, 0, 0)),
        ],
        out_specs=pl.BlockSpec((1, 1, B, D), lambda l, j: (l, j, 0, 0)),
        compiler_params=_params(("arbitrary", "arbitrary")),
        name="ada",
    )(c, ada_w, ada_b.reshape(L, 6, 1, D))
    return jnp.swapaxes(out, 1, 2)


def _even_kernel(x_ref, mod_ref, gn_ref, win_ref, gv_ref, ws_ref, bs_ref, wp_ref, ls_ref,
                 wout_ref, o_ref, ext_ref, mix_ref, *, ts):
    si = pl.program_id(1)
    x = x_ref[0]
    sh1 = mod_ref[0, 0:1, :]
    sc1 = mod_ref[0, 1:2, :]
    g1 = mod_ref[0, 2:3, :]
    h = _norm_mod(x, gn_ref[...], sc1, sh1)
    proj = jnp.dot(h.astype(BF16), win_ref[...], preferred_element_type=F32)

    z = _gelu(proj[:, :2 * GM_WIDTH])
    u = z[:, :GM_WIDTH]
    v = z[:, GM_WIDTH:]
    mu = jnp.mean(v, axis=-1, keepdims=True)
    d = v - mu
    var = jnp.mean(d * d, axis=-1, keepdims=True)
    vn = (d * lax.rsqrt(var + EPS) * gv_ref[...]).astype(BF16)
    r = lax.broadcasted_iota(jnp.int32, (GM_CHUNK, GM_CHUNK), 0)
    cidx = lax.broadcasted_iota(jnp.int32, (GM_CHUNK, GM_CHUNK), 1)
    causal = r >= cidx
    for g in range(GM_GROUPS):
        wg = jnp.where(causal, ws_ref[g], 0.0).astype(BF16)
        for c in range(ts // GM_CHUNK):
            rs = slice(c * GM_CHUNK, (c + 1) * GM_CHUNK)
            cs = slice(g * GM_CH, (g + 1) * GM_CH)
            sv = jnp.dot(wg, vn[rs, cs], preferred_element_type=F32) + bs_ref[g]
            mix_ref[rs, cs] = (u[rs, cs] * sv).astype(BF16)

    p = proj[:, 2 * GM_WIDTH:]

    @pl.when(si == 0)
    def _():
        ext_ref[0:POOL_HALO, :] = jnp.zeros((POOL_HALO, POOL_WIDTH), F32)

    ext_ref[POOL_HALO:POOL_HALO + ts, :] = p
    t = si * ts + lax.broadcasted_iota(jnp.int32, (ts, POOL_CH), 0)
    for g, w in enumerate(POOL_WINDOWS):
        cs = slice(g * POOL_CH, (g + 1) * POOL_CH)
        acc = ext_ref[POOL_HALO:POOL_HALO + ts, cs]
        for k in range(1, w):
            acc = acc + ext_ref[POOL_HALO - k:POOL_HALO - k + ts, cs]
        cnt = jnp.minimum(t + 1, w).astype(F32)
        pooled = acc / cnt - p[:, cs]
        yb = jnp.dot(pooled.astype(BF16), wp_ref[g], preferred_element_type=F32) * ls_ref[:, cs]
        mix_ref[:, GM_WIDTH + g * POOL_CH:GM_WIDTH + (g + 1) * POOL_CH] = yb.astype(BF16)
    ext_ref[0:POOL_HALO, :] = ext_ref[ts:ts + POOL_HALO, :]

    y = jnp.dot(mix_ref[...], wout_ref[...], preferred_element_type=F32)
    o_ref[0] = x + g1 * y


def _even_layer(x, mod, gn, w_in, g_v, w_s, b_s, w_pool, ls, w_out):
    B, S, D = x.shape
    ts = EVEN_TS
    n_in = w_in.shape[1]
    bsb = jnp.broadcast_to(b_s[:, :, None], (GM_GROUPS, GM_CHUNK, GM_CH))
    return pl.pallas_call(
        functools.partial(_even_kernel, ts=ts),
        out_shape=jax.ShapeDtypeStruct((B, S, D), F32),
        grid=(B, S // ts),
        in_specs=[
            pl.BlockSpec((1, ts, D), lambda b, s: (b, s, 0)),
            pl.BlockSpec((1, 6, D), lambda b, s: (b, 0, 0)),
            pl.BlockSpec((1, D), lambda b, s: (0, 0)),
            pl.BlockSpec((D, n_in), lambda b, s: (0, 0)),
            pl.BlockSpec((1, GM_WIDTH), lambda b, s: (0, 0)),
            pl.BlockSpec((GM_GROUPS, GM_CHUNK, GM_CHUNK), lambda b, s: (0, 0, 0)),
            pl.BlockSpec((GM_GROUPS, GM_CHUNK, GM_CH), lambda b, s: (0, 0, 0)),
            pl.BlockSpec((len(POOL_WINDOWS), POOL_CH, POOL_CH), lambda b, s: (0, 0, 0)),
            pl.BlockSpec((1, POOL_WIDTH), lambda b, s: (0, 0)),
            pl.BlockSpec((GM_WIDTH + POOL_WIDTH, D), lambda b, s: (0, 0)),
        ],
        out_specs=pl.BlockSpec((1, ts, D), lambda b, s: (b, s, 0)),
        scratch_shapes=[
            pltpu.VMEM((POOL_HALO + ts, POOL_WIDTH), F32),
            pltpu.VMEM((ts, GM_WIDTH + POOL_WIDTH), BF16),
        ],
        compiler_params=_params(("arbitrary", "arbitrary")),
        name="even_mixer",
    )(x, mod, gn.reshape(1, D), w_in.astype(BF16), g_v.reshape(1, -1), w_s, bsb,
      w_pool.astype(BF16), ls.reshape(1, -1), w_out.astype(BF16))


def _qkv_kernel(x_ref, mod_ref, gn_ref, wt_ref, cos_ref, sin_ref, qt_ref, k_ref, vt_ref,
                kt_scr, *, d_model):
    x = x_ref[0]
    sh1 = mod_ref[0, 0:1, :]
    sc1 = mod_ref[0, 1:2, :]
    h = _norm_mod(x, gn_ref[...], sc1, sh1)
    ht = h.T.astype(BF16)
    pt = jnp.dot(wt_ref[...], ht, preferred_element_type=F32)
    cos = cos_ref[0]
    sin = sin_ref[0]
    half = DA_QK // 2
    scale = DA_QK ** -0.5 * math.log2(math.e)
    for g in range(2 * DA_HEADS):
        r0 = g * DA_QK
        t1 = pt[r0:r0 + half]
        t2 = pt[r0 + half:r0 + DA_QK]
        qt_ref[0, r0:r0 + half, :] = ((t1 * cos - t2 * sin) * scale).astype(BF16)
        qt_ref[0, r0 + half:r0 + DA_QK, :] = ((t2 * cos + t1 * sin) * scale).astype(BF16)
        k0 = d_model + r0
        t1 = pt[k0:k0 + half]
        t2 = pt[k0 + half:k0 + DA_QK]
        kt_scr[r0:r0 + half, :] = t1 * cos - t2 * sin
        kt_scr[r0 + half:r0 + DA_QK, :] = t2 * cos + t1 * sin
    k_ref[0] = kt_scr[...].T.astype(BF16)
    vt_ref[0] = pt[2 * d_model:3 * d_model].astype(BF16)


def _qkv(x, mod, gn, w_in, cos_t, sin_t):
    B, S, D = x.shape
    ts = QKV_TS
    wt = w_in.T.astype(BF16)
    return pl.pallas_call(
        functools.partial(_qkv_kernel, d_model=D),
        out_shape=(
            jax.ShapeDtypeStruct((B, D, S), BF16),
            jax.ShapeDtypeStruct((B, S, D), BF16),
            jax.ShapeDtypeStruct((B, D, S), BF16),
        ),
        grid=(B, S // ts),
        in_specs=[
            pl.BlockSpec((1, ts, D), lambda b, s: (b, s, 0)),
            pl.BlockSpec((1, 6, D), lambda b, s: (b, 0, 0)),
            pl.BlockSpec((1, D), lambda b, s: (0, 0)),
            pl.BlockSpec((3 * D, D), lambda b, s: (0, 0)),
            pl.BlockSpec((1, DA_QK // 2, ts), lambda b, s: (b, 0, s)),
            pl.BlockSpec((1, DA_QK // 2, ts), lambda b, s: (b, 0, s)),
        ],
        out_specs=(
            pl.BlockSpec((1, D, ts), lambda b, s: (b, 0, s)),
            pl.BlockSpec((1, ts, D), lambda b, s: (b, s, 0)),
            pl.BlockSpec((1, D, ts), lambda b, s: (b, 0, s)),
        ),
        scratch_shapes=[pltpu.VMEM((D, ts), F32)],
        compiler_params=_params(("arbitrary", "arbitrary")),
        name="qkv_rope",
    )(x, mod, gn.reshape(1, D), wt, cos_t, sin_t)


def _attn_kernel(qt_ref, k_ref, vt_ref, lam_ref, gs_ref, o_ref, *, tq, tk, tk_big, heads,
                 lam_init):
    qi = pl.program_id(2)
    hd = 2 * DA_QK
    lv = lam_ref[...]
    la = jnp.sum(lv[0:1] * lv[1:2], axis=-1, keepdims=True)
    lb = jnp.sum(lv[2:3] * lv[3:4], axis=-1, keepdims=True)
    lam = jnp.exp(la) - jnp.exp(lb) + lam_init
    row = lax.broadcasted_iota(jnp.int32, (hd, tq), 0)

    def one_head(g):
        qt = qt_ref[0, g * hd:(g + 1) * hd, :]
        zero = jnp.zeros_like(qt)
        q12 = jnp.concatenate(
            [jnp.where(row < DA_QK, qt, zero), jnp.where(row >= DA_QK, qt, zero)], axis=1)

        def update(start, size, masked, carry):
            m_old, l_old, acc_old = carry
            off = pl.multiple_of(start, tk)
            kb = k_ref[0, pl.ds(off, size), g * hd:(g + 1) * hd]
            vb = vt_ref[0, g * DA_V:(g + 1) * DA_V, pl.ds(off, size)]
            s = jnp.dot(kb, q12, preferred_element_type=F32)
            if masked:
                kpos = off + lax.broadcasted_iota(jnp.int32, s.shape, 0)
                lane = lax.broadcasted_iota(jnp.int32, s.shape, 1)
                qpos = qi * tq + jnp.where(lane >= tq, lane - tq, lane)
                s = jnp.where(kpos <= qpos, s, NEG_INF)
            m_new = jnp.maximum(m_old, jnp.max(s, axis=0, keepdims=True))
            alpha = jnp.exp2(m_old - m_new)
            p = jnp.exp2(s - m_new)
            l_new = alpha * l_old + jnp.sum(p, axis=0, keepdims=True)
            acc_new = alpha * acc_old + jnp.dot(vb, p.astype(BF16), preferred_element_type=F32)
            return m_new, l_new, acc_new

        init = (jnp.full((1, 2 * tq), NEG_INF, F32), jnp.zeros((1, 2 * tq), F32),
                jnp.zeros((DA_V, 2 * tq), F32))
        visible = qi * tq
        n_big = visible // tk_big
        carry = lax.fori_loop(0, n_big, lambda j, c: update(j * tk_big, tk_big, False, c), init)
        carry = lax.cond(visible - n_big * tk_big >= tk,
                         lambda c: update(n_big * tk_big, tk, False, c), lambda c: c, carry)
        _, l, acc = update(visible, tk, True, carry)
        o = acc[:, :tq] / l[:, :tq] - lam * (acc[:, tq:] / l[:, tq:])
        ms = jnp.mean(o * o, axis=0, keepdims=True)
        o = o * lax.rsqrt(ms + EPS) * gs_ref[...] * (1.0 - lam_init)
        o_ref[0, g * DA_V:(g + 1) * DA_V, :] = o.astype(BF16)

    for g in range(heads):
        one_head(g)


def _attention(qt, k, vt, lam_vecs, g_sub, lam_init):
    B, D, S = qt.shape
    tq, tk, tk_big, hg = ATT_TQ, ATT_TK, ATT_TK_BIG, ATT_HEADS
    assert tk == tq and tk_big == 2 * tk and S % tk == 0 and DA_HEADS % hg == 0
    hd = 2 * DA_QK
    return pl.pallas_call(
        functools.partial(_attn_kernel, tq=tq, tk=tk, tk_big=tk_big, heads=hg,
                          lam_init=lam_init),
        out_shape=jax.ShapeDtypeStruct((B, D, S), BF16),
        grid=(B, DA_HEADS // hg, S // tq),
        in_specs=[
            pl.BlockSpec((1, hg * hd, tq), lambda b, h, q: (b, h, q)),
            pl.BlockSpec((1, S, hg * hd), lambda b, h, q: (b, 0, h)),
            pl.BlockSpec((1, hg * DA_V, S), lambda b, h, q: (b, h, 0)),
            pl.BlockSpec((4, DA_QK), lambda b, h, q: (0, 0)),
            pl.BlockSpec((DA_V, 1), lambda b, h, q: (0, 0)),
        ],
        out_specs=pl.BlockSpec((1, hg * DA_V, tq), lambda b, h, q: (b, h, q)),
        compiler_params=_params(("arbitrary", "arbitrary", "arbitrary")),
        name="diff_attention",
    )(qt, k, vt, lam_vecs, g_sub.reshape(DA_V, 1))


def _oproj_kernel(ot_ref, wt_ref, x_ref, mod_ref, o_ref):
    yt = jnp.dot(wt_ref[...], ot_ref[0], preferred_element_type=F32)
    g1 = mod_ref[0, 2:3, :]
    o_ref[0] = x_ref[0] + g1 * yt.T


def _oproj(ot, w_out, x, mod):
    B, S, D = x.shape
    ts = QKV_TS
    return pl.pallas_call(
        _oproj_kernel,
        out_shape=jax.ShapeDtypeStruct((B, S, D), F32),
        grid=(B, S // ts),
        in_specs=[
            pl.BlockSpec((1, D, ts), lambda b, s: (b, 0, s)),
            pl.BlockSpec((D, D), lambda b, s: (0, 0)),
            pl.BlockSpec((1, ts, D), lambda b, s: (b, s, 0)),
            pl.BlockSpec((1, 6, D), lambda b, s: (b, 0, 0)),
        ],
        out_specs=pl.BlockSpec((1, ts, D), lambda b, s: (b, s, 0)),
        compiler_params=_params(("arbitrary", "arbitrary")),
        name="attn_out_proj",
    )(ot, w_out.T.astype(BF16), x, mod)


def _top_pair_sums(a, b, k):
    n = F32_SUBLANES
    assert k == 2 * n and len(a) == k and len(b) == k
    b_lo = jnp.concatenate(b[:n], axis=0)
    lists = [a[p] + b_lo for p in range(n)]
    row0 = a[0] + jnp.concatenate(b[n:], axis=0)
    col0 = jnp.concatenate(a[n:], axis=0) + b[0]
    vals = []
    for _ in range(k):
        head = jnp.maximum(jnp.maximum(lists[0], row0), col0)
        m = jnp.max(head, axis=0, keepdims=True)
        vals.append(m)
        hit = lists[0] == m
        for r in range(n - 1):
            lists[r] = jnp.where(hit, lists[r + 1], lists[r])
        lists[n - 1] = jnp.where(hit, NEG_INF, lists[n - 1])
        row0 = jnp.where(row0 == m, NEG_INF, row0)
        col0 = jnp.where(col0 == m, NEG_INF, col0)
    return vals


def _rank_among(s, b):
    assert len(b) == 16
    lo8 = s >= b[7]
    lo4 = s >= jnp.where(lo8, b[3], b[11])
    lo2 = s >= jnp.where(lo8, jnp.where(lo4, b[1], b[5]), jnp.where(lo4, b[9], b[13]))
    lo1 = s >= jnp.where(
        lo8,
        jnp.where(lo4, jnp.where(lo2, b[0], b[2]), jnp.where(lo2, b[4], b[6])),
        jnp.where(lo4, jnp.where(lo2, b[8], b[10]), jnp.where(lo2, b[12], b[14])))
    rank = (jnp.where(lo8, 0.0, 8.0) + jnp.where(lo4, 0.0, 4.0)
            + jnp.where(lo2, 0.0, 2.0) + jnp.where(lo1, 0.0, 1.0))
    return jnp.where(s >= b[15], rank, 16.0)


def _top_values_128(s, k):
    n = s.shape[0] // 8
    assert n == 16 and k <= n
    blocks = [s[8 * v:8 * v + 8, :] for v in range(n)]
    size = 2
    while size <= n:
        stride = size // 2
        while stride >= 1:
            for i in range(n):
                j = i ^ stride
                if j > i:
                    hi = jnp.maximum(blocks[i], blocks[j])
                    lo = jnp.minimum(blocks[i], blocks[j])
                    blocks[i], blocks[j] = (hi, lo) if (i & size) == 0 else (lo, hi)
            stride //= 2
        size *= 2
    vals = []
    for t in range(k):
        head = blocks[0]
        m = jnp.max(head, axis=0, keepdims=True)
        vals.append(m)
        hit = head == m
        for r in range(k - 1 - t):
            blocks[r] = jnp.where(hit, blocks[r + 1], blocks[r])
    return vals


def _route_kernel(x_ref, mod_ref, gn_ref, wq_ref, sk_ref, ht_ref, cnt_ref, e1_ref, r2_ref, e2_ref):
    x = x_ref[...]
    sh2 = mod_ref[0, 3:4, :]
    sc2 = mod_ref[0, 4:5, :]
    h = _norm_mod(x, gn_ref[...], sc2, sh2)
    ht_ref[...] = h.T.astype(BF16)
    q = jnp.dot(h.astype(BF16), wq_ref[...], preferred_element_type=F32)
    K = PEER_TOPK
    nk = PEER_NKEYS
    for hd in range(PEER_HEADS):
        rs = slice(hd * nk, (hd + 1) * nk)
        s12 = []
        for p in range(2):
            c0 = (hd * 2 + p) * nk
            qhp = q[:, c0:c0 + nk].astype(BF16)
            s12.append(lax.dot_general(sk_ref[hd * 2 + p], qhp, (((1,), (1,)), ((), ())),
                                       preferred_element_type=F32))
        s1, s2 = s12
        a = _top_values_128(s1, K)
        b = _top_values_128(s2, K)
        r2 = _rank_among(s2, b)
        c = _top_pair_sums(a, b, K)
        tau = c[K - 1]
        zsum = jnp.ones_like(c[0])
        for k in range(1, K):
            zsum = zsum + jnp.exp(c[k] - c[0])
        a_all = jnp.concatenate(a, axis=0)
        ck = jnp.zeros(a_all.shape, F32)
        for l in range(K):
            ck = ck + jnp.where((a_all + b[l]) >= tau, 1.0, 0.0)
        cnt = jnp.zeros(s1.shape, F32)
        for k in range(K):
            cnt = jnp.where(s1 == a[k], ck[k:k + 1, :], cnt)
        cnt_ref[rs, :] = cnt
        e1_ref[rs, :] = jnp.exp(s1 - a[0])
        r2_ref[rs, :] = r2.astype(BF16)
        e2_ref[rs, :] = (0.5 * jnp.exp(s2 - b[0]) / zsum).astype(BF16)


def _dense_kernel(ht_ref, u_ref, vt_ref, cnt_ref, e1_ref, r2_ref, e2_ref, x_ref, mod_ref, fn_ref,
                  o_ref, acc_scr, *wa_scrs, sub_size, chunk, n_steps, final_norm):
    ec = pl.program_id(1)
    nk = PEER_NKEYS
    per = sub_size // nk
    group = F32_SUBLANES
    T = ht_ref.shape[1]

    @pl.when(ec == 0)
    def _():
        acc_scr[...] = jnp.zeros(acc_scr.shape, F32)

    def gate_block(i_rel):
        w = jnp.zeros((nk, T), BF16)
        grp, ii = divmod(i_rel, group)
        for hd in range(PEER_HEADS):
            rs = slice(hd * nk, (hd + 1) * nk)
            base = pl.multiple_of(hd * nk + ec * (len(wa_scrs) * per) + grp * group, group)
            cnt_row = cnt_ref[pl.ds(base, group), :][ii:ii + 1, :]
            e1_row = e1_ref[pl.ds(base, group), :][ii:ii + 1, :]
            cb = jnp.broadcast_to(cnt_row, (nk, T)).astype(BF16)
            eb = jnp.broadcast_to(e1_row, (nk, T)).astype(BF16)
            sel = r2_ref[rs, :] < cb
            w = w + jnp.where(sel, e2_ref[rs, :], jnp.zeros((), BF16)) * eb
        return w

    total = None
    for sidx, wa_scr in enumerate(wa_scrs):
        for c in range(per // chunk):
            rows = slice(c * chunk * nk, (c + 1) * chunk * nk)
            urows = slice(sidx * sub_size + c * chunk * nk, sidx * sub_size + (c + 1) * chunk * nk)
            act = jnp.dot(u_ref[urows, :], ht_ref[...], preferred_element_type=F32)
            ab = act.astype(BF16)
            g = ab * (1.0 + lax.erf(ab * 0.7071067811865476))
            w = jnp.concatenate([gate_block(sidx * per + c * chunk + j) for j in range(chunk)],
                                axis=0)
            wa_scr[rows, :] = w * g
        y = jnp.dot(vt_ref[:, sidx * sub_size:(sidx + 1) * sub_size], wa_scr[...],
                    preferred_element_type=F32)
        total = y if total is None else total + y
    acc_scr[...] += total

    @pl.when(ec == n_steps - 1)
    def _():
        g2 = mod_ref[0, 5:6, :]
        y = x_ref[...] + g2 * acc_scr[...].T
        if final_norm:
            ms = jnp.mean(y * y, axis=-1, keepdims=True)
            y = y * lax.rsqrt(ms + EPS) * fn_ref[...]
        o_ref[...] = y


def _peer(x, mod, gn, w_q, sub_keys, u_bf, vt_bf, fn=None):
    B, S, D = x.shape
    N = B * S
    H, NK = PEER_HEADS, PEER_NKEYS
    E = u_bf.shape[0]
    tr = ROUTE_T
    x2 = x.reshape(N, D)
    wq = w_q.reshape(D, H * 2 * NK).astype(BF16)
    sk = sub_keys.reshape(H * 2, NK, NK).astype(BF16)
    ht, cnt, e1, r2, e2 = pl.pallas_call(
        _route_kernel,
        out_shape=(
            jax.ShapeDtypeStruct((D, N), BF16),
            jax.ShapeDtypeStruct((H * NK, N), F32),
            jax.ShapeDtypeStruct((H * NK, N), F32),
            jax.ShapeDtypeStruct((H * NK, N), BF16),
            jax.ShapeDtypeStruct((H * NK, N), BF16),
        ),
        grid=(N // tr,),
        in_specs=[
            pl.BlockSpec((tr, D), lambda i: (i, 0)),
            pl.BlockSpec((1, 6, D), lambda i: ((i * tr) // S, 0, 0)),
            pl.BlockSpec((1, D), lambda i: (0, 0)),
            pl.BlockSpec((D, H * 2 * NK), lambda i: (0, 0)),
            pl.BlockSpec((H * 2, NK, NK), lambda i: (0, 0, 0)),
        ],
        out_specs=(
            pl.BlockSpec((D, tr), lambda i: (0, i)),
            pl.BlockSpec((H * NK, tr), lambda i: (0, i)),
            pl.BlockSpec((H * NK, tr), lambda i: (0, i)),
            pl.BlockSpec((H * NK, tr), lambda i: (0, i)),
            pl.BlockSpec((H * NK, tr), lambda i: (0, i)),
        ),
        compiler_params=_params(("arbitrary",)),
        name="peer_route",
    )(x2, mod, gn.reshape(1, D), wq, sk)

    td, sub, nsub = DENSE_T, DENSE_SUB, DENSE_NSUB
    ec = sub * nsub
    assert (sub // NK) % 8 == 0 and (sub // NK) % DENSE_CHUNK == 0 and E % ec == 0
    assert S % td == 0
    rank_spec = pl.BlockSpec((H * NK, td), lambda t, e: (0, t))
    fn_arr = jnp.ones((1, D), F32) if fn is None else fn.reshape(1, D)
    out = pl.pallas_call(
        functools.partial(_dense_kernel, sub_size=sub, chunk=DENSE_CHUNK, n_steps=E // ec,
                          final_norm=fn is not None),
        out_shape=jax.ShapeDtypeStruct((N, D), F32),
        grid=(N // td, E // ec),
        in_specs=[
            pl.BlockSpec((D, td), lambda t, e: (0, t)),
            pl.BlockSpec((ec, D), lambda t, e: (e, 0)),
            pl.BlockSpec((D, ec), lambda t, e: (0, e)),
            rank_spec, rank_spec, rank_spec, rank_spec,
            pl.BlockSpec((td, D), lambda t, e: (t, 0)),
            pl.BlockSpec((1, 6, D), lambda t, e: ((t * td) // S, 0, 0)),
            pl.BlockSpec((1, D), lambda t, e: (0, 0)),
        ],
        out_specs=pl.BlockSpec((td, D), lambda t, e: (t, 0)),
        scratch_shapes=[pltpu.VMEM((D, td), F32)] + [pltpu.VMEM((sub, td), BF16)] * nsub,
        compiler_params=_params(("arbitrary", "arbitrary")),
        name="peer_experts",
    )(ht, u_bf, vt_bf, cnt, e1, r2, e2, x2, mod, fn_arr)
    return out.reshape(B, S, D)


def kernel(x, c, positions, ada_w, ada_b, norm_mix, norm_ffn, ev_w_in, ev_g_v, ev_w_s, ev_b_s,
           ev_w_pool, ev_pool_scale, ev_w_out, od_w_in, od_lam_q1, od_lam_k1, od_lam_q2,
           od_lam_k2, od_g_sub, od_w_out, peer_w_q, peer_sub_keys, peer_u, peer_v, final_norm):
    B, S, D = x.shape
    depth = ada_w.shape[0]
    inv_freq = 1.0 / (ROPE_THETA ** (jnp.arange(0, DA_QK, 2, dtype=F32) / DA_QK))
    ang = positions.astype(F32)[..., None] * inv_freq
    cos_t = jnp.swapaxes(jnp.cos(ang), 1, 2)
    sin_t = jnp.swapaxes(jnp.sin(ang), 1, 2)
    mods = _ada(c, ada_w, ada_b)
    for l in range(depth):
        mod = mods[l]
        if l % 2 == 0:
            e = l // 2
            x = _even_layer(x, mod, norm_mix[l], ev_w_in[e], ev_g_v[e], ev_w_s[e], ev_b_s[e],
                            ev_w_pool[e], ev_pool_scale[e], ev_w_out[e])
        else:
            o = l // 2
            lam_init = 0.8 - 0.6 * math.exp(-0.3 * l)
            qt, k, vt = _qkv(x, mod, norm_mix[l], od_w_in[o], cos_t, sin_t)
            lam_vecs = jnp.stack([od_lam_q1[o], od_lam_k1[o], od_lam_q2[o], od_lam_k2[o]])
            ot = _attention(qt, k, vt, lam_vecs, od_g_sub[o], lam_init)
            x = _oproj(ot, od_w_out[o], x, mod)
        x = _peer(x, mod, norm_ffn[l], peer_w_q[l], peer_sub_keys[l],
                  peer_u[l].astype(BF16), peer_v[l].T.astype(BF16),
                  fn=final_norm if l == depth - 1 else None)
    return x
```

```python
import functools
import math

import jax
import jax.numpy as jnp
from jax import lax
from jax.experimental import pallas as pl
from jax.experimental.pallas import tpu as pltpu

F32 = jnp.float32
BF16 = jnp.bfloat16
EPS = 1e-6
NEG_INF = float("-inf")

GM_GROUPS = 4
GM_CH = 128
GM_WIDTH = GM_GROUPS * GM_CH
GM_CHUNK = 128
POOL_WINDOWS = (2, 4, 8, 16)
POOL_CH = 128
POOL_WIDTH = len(POOL_WINDOWS) * POOL_CH
POOL_HALO = max(POOL_WINDOWS)
DA_HEADS = 8
DA_QK = 64
DA_V = 128
ROPE_THETA = 10000.0
PEER_HEADS = 8
PEER_NKEYS = 128
PEER_TOPK = 16

F32_SUBLANES = 8

V7X_VMEM_LIMIT = 56 * 1024 * 1024

EVEN_TS = 512
QKV_TS = 512
ATT_TQ = 512
ATT_TK = 512
ATT_TK_BIG = 1024
ATT_HEADS = 1
ROUTE_T = 256
DENSE_T = 512
DENSE_SUB = 2048
DENSE_NSUB = 1
DENSE_CHUNK = 2


def _params(sem):
    return pltpu.CompilerParams(dimension_semantics=sem, vmem_limit_bytes=V7X_VMEM_LIMIT)


def _gelu(x):
    return 0.5 * x * (1.0 + lax.erf(x * 0.7071067811865476))


def _norm_mod(x, gn, sc, sh):
    ms = jnp.mean(x * x, axis=-1, keepdims=True)
    return x * lax.rsqrt(ms + EPS) * gn * (1.0 + sc) + sh


def _ada_kernel(c_ref, w_ref, b_ref, o_ref):
    c = c_ref[...]
    ca = c * jax.nn.sigmoid(c)
    o_ref[0, 0] = jnp.dot(ca, w_ref[0], preferred_element_type=F32,
                          precision=lax.Precision.HIGHEST) + b_ref[0, 0]


def _ada(c, ada_w, ada_b):
    L, D, _ = ada_w.shape
    B = c.shape[0]
    out = pl.pallas_call(
        _ada_kernel,
        out_shape=jax.ShapeDtypeStruct((L, 6, B, D), F32),
        grid=(L, 6),
        in_specs=[
            pl.BlockSpec((B, D), lambda l, j: (0, 0)),
            pl.BlockSpec((1, D, D), lambda l, j: (l, 0, j)),
            pl.BlockSpec((1, 1, 1, D), lambda l, j: (l, j, 0, 0)),
        ],
        out_specs=pl.BlockSpec((1, 1, B, D), lambda l, j: (l, j, 0, 0)),
        compiler_params=_params(("arbitrary", "arbitrary")),
        name="ada",
    )(c, ada_w, ada_b.reshape(L, 6, 1, D))
    return jnp.swapaxes(out, 1, 2)


def _even_kernel(x_ref, mod_ref, gn_ref, win_ref, gv_ref, ws_ref, bs_ref, wp_ref, ls_ref,
                 wout_ref, o_ref, ext_ref, mix_ref, *, ts):
    si = pl.program_id(1)
    x = x_ref[0]
    sh1 = mod_ref[0, 0:1, :]
    sc1 = mod_ref[0, 1:2, :]
    g1 = mod_ref[0, 2:3, :]
    h = _norm_mod(x, gn_ref[...], sc1, sh1)
    proj = jnp.dot(h.astype(BF16), win_ref[...], preferred_element_type=F32)

    z = _gelu(proj[:, :2 * GM_WIDTH])
    u = z[:, :GM_WIDTH]
    v = z[:, GM_WIDTH:]
    mu = jnp.mean(v, axis=-1, keepdims=True)
    d = v - mu
    var = jnp.mean(d * d, axis=-1, keepdims=True)
    vn = (d * lax.rsqrt(var + EPS) * gv_ref[...]).astype(BF16)
    r = lax.broadcasted_iota(jnp.int32, (GM_CHUNK, GM_CHUNK), 0)
    cidx = lax.broadcasted_iota(jnp.int32, (GM_CHUNK, GM_CHUNK), 1)
    causal = r >= cidx
    for g in range(GM_GROUPS):
        wg = jnp.where(causal, ws_ref[g], 0.0).astype(BF16)
        for c in range(ts // GM_CHUNK):
            rs = slice(c * GM_CHUNK, (c + 1) * GM_CHUNK)
            cs = slice(g * GM_CH, (g + 1) * GM_CH)
            sv = jnp.dot(wg, vn[rs, cs], preferred_element_type=F32) + bs_ref[g]
            mix_ref[rs, cs] = (u[rs, cs] * sv).astype(BF16)

    p = proj[:, 2 * GM_WIDTH:]

    @pl.when(si == 0)
    def _():
        ext_ref[0:POOL_HALO, :] = jnp.zeros((POOL_HALO, POOL_WIDTH), F32)

    ext_ref[POOL_HALO:POOL_HALO + ts, :] = p
    t = si * ts + lax.broadcasted_iota(jnp.int32, (ts, POOL_CH), 0)
    for g, w in enumerate(POOL_WINDOWS):
        cs = slice(g * POOL_CH, (g + 1) * POOL_CH)
        acc = ext_ref[POOL_HALO:POOL_HALO + ts, cs]
        for k in range(1, w):
            acc = acc + ext_ref[POOL_HALO - k:POOL_HALO - k + ts, cs]
        cnt = jnp.minimum(t + 1, w).astype(F32)
        pooled = acc / cnt - p[:, cs]
        yb = jnp.dot(pooled.astype(BF16), wp_ref[g], preferred_element_type=F32) * ls_ref[:, cs]
        mix_ref[:, GM_WIDTH + g * POOL_CH:GM_WIDTH + (g + 1) * POOL_CH] = yb.astype(BF16)
    ext_ref[0:POOL_HALO, :] = ext_ref[ts:ts + POOL_HALO, :]

    y = jnp.dot(mix_ref[...], wout_ref[...], preferred_element_type=F32)
    o_ref[0] = x + g1 * y


def _even_layer(x, mod, gn, w_in, g_v, w_s, b_s, w_pool, ls, w_out):
    B, S, D = x.shape
    ts = EVEN_TS
    n_in = w_in.shape[1]
    bsb = jnp.broadcast_to(b_s[:, :, None], (GM_GROUPS, GM_CHUNK, GM_CH))
    return pl.pallas_call(
        functools.partial(_even_kernel, ts=ts),
        out_shape=jax.ShapeDtypeStruct((B, S, D), F32),
        grid=(B, S // ts),
        in_specs=[
            pl.BlockSpec((1, ts, D), lambda b, s: (b, s, 0)),
            pl.BlockSpec((1, 6, D), lambda b, s: (b, 0, 0)),
            pl.BlockSpec((1, D), lambda b, s: (0, 0)),
            pl.BlockSpec((D, n_in), lambda b, s: (0, 0)),
            pl.BlockSpec((1, GM_WIDTH), lambda b, s: (0, 0)),
            pl.BlockSpec((GM_GROUPS, GM_CHUNK, GM_CHUNK), lambda b, s: (0, 0, 0)),
            pl.BlockSpec((GM_GROUPS, GM_CHUNK, GM_CH), lambda b, s: (0, 0, 0)),
            pl.BlockSpec((len(POOL_WINDOWS), POOL_CH, POOL_CH), lambda b, s: (0, 0, 0)),
            pl.BlockSpec((1, POOL_WIDTH), lambda b, s: (0, 0)),
            pl.BlockSpec((GM_WIDTH + POOL_WIDTH, D), lambda b, s: (0, 0)),
        ],
        out_specs=pl.BlockSpec((1, ts, D), lambda b, s: (b, s, 0)),
        scratch_shapes=[
            pltpu.VMEM((POOL_HALO + ts, POOL_WIDTH), F32),
            pltpu.VMEM((ts, GM_WIDTH + POOL_WIDTH), BF16),
        ],
        compiler_params=_params(("arbitrary", "arbitrary")),
        name="even_mixer",
    )(x, mod, gn.reshape(1, D), w_in.astype(BF16), g_v.reshape(1, -1), w_s, bsb,
      w_pool.astype(BF16), ls.reshape(1, -1), w_out.astype(BF16))


def _qkv_kernel(x_ref, mod_ref, gn_ref, wt_ref, cos_ref, sin_ref, qt_ref, k_ref, vt_ref,
                kt_scr, *, d_model):
    x = x_ref[0]
    sh1 = mod_ref[0, 0:1, :]
    sc1 = mod_ref[0, 1:2, :]
    h = _norm_mod(x, gn_ref[...], sc1, sh1)
    ht = h.T.astype(BF16)
    pt = jnp.dot(wt_ref[...], ht, preferred_element_type=F32)
    cos = cos_ref[0]
    sin = sin_ref[0]
    half = DA_QK // 2
    scale = DA_QK ** -0.5 * math.log2(math.e)
    for g in range(2 * DA_HEADS):
        r0 = g * DA_QK
        t1 = pt[r0:r0 + half]
        t2 = pt[r0 + half:r0 + DA_QK]
        qt_ref[0, r0:r0 + half, :] = ((t1 * cos - t2 * sin) * scale).astype(BF16)
        qt_ref[0, r0 + half:r0 + DA_QK, :] = ((t2 * cos + t1 * sin) * scale).astype(BF16)
        k0 = d_model + r0
        t1 = pt[k0:k0 + half]
        t2 = pt[k0 + half:k0 + DA_QK]
        kt_scr[r0:r0 + half, :] = t1 * cos - t2 * sin
        kt_scr[r0 + half:r0 + DA_QK, :] = t2 * cos + t1 * sin
    k_ref[0] = kt_scr[...].T.astype(BF16)
    vt_ref[0] = pt[2 * d_model:3 * d_model].astype(BF16)


def _qkv(x, mod, gn, w_in, cos_t, sin_t):
    B, S, D = x.shape
    ts = QKV_TS
    wt = w_in.T.astype(BF16)
    return pl.pallas_call(
        functools.partial(_qkv_kernel, d_model=D),
        out_shape=(
            jax.ShapeDtypeStruct((B, D, S), BF16),
            jax.ShapeDtypeStruct((B, S, D), BF16),
            jax.ShapeDtypeStruct((B, D, S), BF16),
        ),
        grid=(B, S // ts),
        in_specs=[
            pl.BlockSpec((1, ts, D), lambda b, s: (b, s, 0)),
            pl.BlockSpec((1, 6, D), lambda b, s: (b, 0, 0)),
            pl.BlockSpec((1, D), lambda b, s: (0, 0)),
            pl.BlockSpec((3 * D, D), lambda b, s: (0, 0)),
            pl.BlockSpec((1, DA_QK // 2, ts), lambda b, s: (b, 0, s)),
            pl.BlockSpec((1, DA_QK // 2, ts), lambda b, s: (b, 0, s)),
        ],
        out_specs=(
            pl.BlockSpec((1, D, ts), lambda b, s: (b, 0, s)),
            pl.BlockSpec((1, ts, D), lambda b, s: (b, s, 0)),
            pl.BlockSpec((1, D, ts), lambda b, s: (b, 0, s)),
        ),
        scratch_shapes=[pltpu.VMEM((D, ts), F32)],
        compiler_params=_params(("arbitrary", "arbitrary")),
        name="qkv_rope",
    )(x, mod, gn.reshape(1, D), wt, cos_t, sin_t)


def _attn_kernel(qt_ref, k_ref, vt_ref, lam_ref, gs_ref, o_ref, *, tq, tk, tk_big, heads,
                 lam_init):
    qi = pl.program_id(2)
    hd = 2 * DA_QK
    lv = lam_ref[...]
    la = jnp.sum(lv[0:1] * lv[1:2], axis=-1, keepdims=True)
    lb = jnp.sum(lv[2:3] * lv[3:4], axis=-1, keepdims=True)
    lam = jnp.exp(la) - jnp.exp(lb) + lam_init
    row = lax.broadcasted_iota(jnp.int32, (hd, tq), 0)

    def one_head(g):
        qt = qt_ref[0, g * hd:(g + 1) * hd, :]
        zero = jnp.zeros_like(qt)
        q12 = jnp.concatenate(
            [jnp.where(row < DA_QK, qt, zero), jnp.where(row >= DA_QK, qt, zero)], axis=1)

        def update(start, size, masked, carry):
            m_old, l_old, acc_old = carry
            off = pl.multiple_of(start, tk)
            kb = k_ref[0, pl.ds(off, size), g * hd:(g + 1) * hd]
            vb = vt_ref[0, g * DA_V:(g + 1) * DA_V, pl.ds(off, size)]
            s = jnp.dot(kb, q12, preferred_element_type=F32)
            if masked:
                kpos = off + lax.broadcasted_iota(jnp.int32, s.shape, 0)
                lane = lax.broadcasted_iota(jnp.int32, s.shape, 1)
                qpos = qi * tq + jnp.where(lane >= tq, lane - tq, lane)
                s = jnp.where(kpos <= qpos, s, NEG_INF)
            m_new = jnp.maximum(m_old, jnp.max(s, axis=0, keepdims=True))
            alpha = jnp.exp2(m_old - m_new)
            p = jnp.exp2(s - m_new)
            l_new = alpha * l_old + jnp.sum(p, axis=0, keepdims=True)
            acc_new = alpha * acc_old + jnp.dot(vb, p.astype(BF16), preferred_element_type=F32)
            return m_new, l_new, acc_new

        init = (jnp.full((1, 2 * tq), NEG_INF, F32), jnp.zeros((1, 2 * tq), F32),
                jnp.zeros((DA_V, 2 * tq), F32))
        visible = qi * tq
        n_big = visible // tk_big
        carry = lax.fori_loop(0, n_big, lambda j, c: update(j * tk_big, tk_big, False, c), init)
        carry = lax.cond(visible - n_big * tk_big >= tk,
                         lambda c: update(n_big * tk_big, tk, False, c), lambda c: c, carry)
        _, l, acc = update(visible, tk, True, carry)
        o = acc[:, :tq] / l[:, :tq] - lam * (acc[:, tq:] / l[:, tq:])
        ms = jnp.mean(o * o, axis=0, keepdims=True)
        o = o * lax.rsqrt(ms + EPS) * gs_ref[...] * (1.0 - lam_init)
        o_ref[0, g * DA_V:(g + 1) * DA_V, :] = o.astype(BF16)

    for g in range(heads):
        one_head(g)


def _attention(qt, k, vt, lam_vecs, g_sub, lam_init):
    B, D, S = qt.shape
    tq, tk, tk_big, hg = ATT_TQ, ATT_TK, ATT_TK_BIG, ATT_HEADS
    assert tk == tq and tk_big == 2 * tk and S % tk == 0 and DA_HEADS % hg == 0
    hd = 2 * DA_QK
    return pl.pallas_call(
        functools.partial(_attn_kernel, tq=tq, tk=tk, tk_big=tk_big, heads=hg,
                          lam_init=lam_init),
        out_shape=jax.ShapeDtypeStruct((B, D, S), BF16),
        grid=(B, DA_HEADS // hg, S // tq),
        in_specs=[
            pl.BlockSpec((1, hg * hd, tq), lambda b, h, q: (b, h, q)),
            pl.BlockSpec((1, S, hg * hd), lambda b, h, q: (b, 0, h)),
            pl.BlockSpec((1, hg * DA_V, S), lambda b, h, q: (b, h, 0)),
            pl.BlockSpec((4, DA_QK), lambda b, h, q: (0, 0)),
            pl.BlockSpec((DA_V, 1), lambda b, h, q: (0, 0)),
        ],
        out_specs=pl.BlockSpec((1, hg * DA_V, tq), lambda b, h, q: (b, h, q)),
        compiler_params=_params(("arbitrary", "arbitrary", "arbitrary")),
        name="diff_attention",
    )(qt, k, vt, lam_vecs, g_sub.reshape(DA_V, 1))


def _oproj_kernel(ot_ref, wt_ref, x_ref, mod_ref, o_ref):
    yt = jnp.dot(wt_ref[...], ot_ref[0], preferred_element_type=F32)
    g1 = mod_ref[0, 2:3, :]
    o_ref[0] = x_ref[0] + g1 * yt.T


def _oproj(ot, w_out, x, mod):
    B, S, D = x.shape
    ts = QKV_TS
    return pl.pallas_call(
        _oproj_kernel,
        out_shape=jax.ShapeDtypeStruct((B, S, D), F32),
        grid=(B, S // ts),
        in_specs=[
            pl.BlockSpec((1, D, ts), lambda b, s: (b, 0, s)),
            pl.BlockSpec((D, D), lambda b, s: (0, 0)),
            pl.BlockSpec((1, ts, D), lambda b, s: (b, s, 0)),
            pl.BlockSpec((1, 6, D), lambda b, s: (b, 0, 0)),
        ],
        out_specs=pl.BlockSpec((1, ts, D), lambda b, s: (b, s, 0)),
        compiler_params=_params(("arbitrary", "arbitrary")),
        name="attn_out_proj",
    )(ot, w_out.T.astype(BF16), x, mod)


def _top_pair_sums(a, b, k):
    n = F32_SUBLANES
    assert k == 2 * n and len(a) == k and len(b) == k
    b_lo = jnp.concatenate(b[:n], axis=0)
    lists = [a[p] + b_lo for p in range(n)]
    row0 = a[0] + jnp.concatenate(b[n:], axis=0)
    col0 = jnp.concatenate(a[n:], axis=0) + b[0]
    vals = []
    for _ in range(k):
        head = jnp.maximum(jnp.maximum(lists[0], row0), col0)
        m = jnp.max(head, axis=0, keepdims=True)
        vals.append(m)
        hit = lists[0] == m
        for r in range(n - 1):
            lists[r] = jnp.where(hit, lists[r + 1], lists[r])
        lists[n - 1] = jnp.where(hit, NEG_INF, lists[n - 1])
        row0 = jnp.where(row0 == m, NEG_INF, row0)
        col0 = jnp.where(col0 == m, NEG_INF, col0)
    return vals


def _rank_among(s, b):
    assert len(b) == 16
    lo8 = s >= b[7]
    lo4 = s >= jnp.where(lo8, b[3], b[11])
    lo2 = s >= jnp.where(lo8, jnp.where(lo4, b[1], b[5]), jnp.where(lo4, b[9], b[13]))
    lo1 = s >= jnp.where(
        lo8,
        jnp.where(lo4, jnp.where(lo2, b[0], b[2]), jnp.where(lo2, b[4], b[6])),
        jnp.where(lo4, jnp.where(lo2, b[8], b[10]), jnp.where(lo2, b[12], b[14])))
    rank = (jnp.where(lo8, 0.0, 8.0) + jnp.where(lo4, 0.0, 4.0)
            + jnp.where(lo2, 0.0, 2.0) + jnp.where(lo1, 0.0, 1.0))
    return jnp.where(s >= b[15], rank, 16.0)


def _top_values_128(s, k):
    n = s.shape[0] // 8
    assert n == 16 and k <= n
    blocks = [s[8 * v:8 * v + 8, :] for v in range(n)]
    size = 2
    while size <= n:
        stride = size // 2
        while stride >= 1:
            for i in range(n):
                j = i ^ stride
                if j > i:
                    hi = jnp.maximum(blocks[i], blocks[j])
                    lo = jnp.minimum(blocks[i], blocks[j])
                    blocks[i], blocks[j] = (hi, lo) if (i & size) == 0 else (lo, hi)
            stride //= 2
        size *= 2
    vals = []
    for t in range(k):
        head = blocks[0]
        m = jnp.max(head, axis=0, keepdims=True)
        vals.append(m)
        hit = head == m
        for r in range(k - 1 - t):
            blocks[r] = jnp.where(hit, blocks[r + 1], blocks[r])
    return vals


def _route_kernel(x_ref, mod_ref, gn_ref, wq_ref, sk_ref, ht_ref, cnt_ref, e1_ref, r2_ref, e2_ref):
    x = x_ref[...]
    sh2 = mod_ref[0, 3:4, :]
    sc2 = mod_ref[0, 4:5, :]
    h = _norm_mod(x, gn_ref[...], sc2, sh2)
    ht_ref[...] = h.T.astype(BF16)
    q = jnp.dot(h.astype(BF16), wq_ref[...], preferred_element_type=F32)
    K = PEER_TOPK
    nk = PEER_NKEYS
    for hd in range(PEER_HEADS):
        rs = slice(hd * nk, (hd + 1) * nk)
        s12 = []
        for p in range(2):
            c0 = (hd * 2 + p) * nk
            qhp = q[:, c0:c0 + nk].astype(BF16)
            s12.append(lax.dot_general(sk_ref[hd * 2 + p], qhp, (((1,), (1,)), ((), ())),
                                       preferred_element_type=F32))
        s1, s2 = s12
        a = _top_values_128(s1, K)
        b = _top_values_128(s2, K)
        r2 = _rank_among(s2, b)
        c = _top_pair_sums(a, b, K)
        tau = c[K - 1]
        zsum = jnp.ones_like(c[0])
        for k in range(1, K):
            zsum = zsum + jnp.exp(c[k] - c[0])
        a_all = jnp.concatenate(a, axis=0)
        ck = jnp.zeros(a_all.shape, F32)
        for l in range(K):
            ck = ck + jnp.where((a_all + b[l]) >= tau, 1.0, 0.0)
        cnt = jnp.zeros(s1.shape, F32)
        for k in range(K):
            cnt = jnp.where(s1 == a[k], ck[k:k + 1, :], cnt)
        cnt_ref[rs, :] = cnt
        e1_ref[rs, :] = jnp.exp(s1 - a[0])
        r2_ref[rs, :] = r2.astype(BF16)
        e2_ref[rs, :] = (0.5 * jnp.exp(s2 - b[0]) / zsum).astype(BF16)


def _dense_kernel(ht_ref, u_ref, vt_ref, cnt_ref, e1_ref, r2_ref, e2_ref, x_ref, mod_ref, fn_ref,
                  o_ref, acc_scr, *wa_scrs, sub_size, chunk, n_steps, final_norm):
    ec = pl.program_id(1)
    nk = PEER_NKEYS
    per = sub_size // nk
    group = F32_SUBLANES
    T = ht_ref.shape[1]

    @pl.when(ec == 0)
    def _():
        acc_scr[...] = jnp.zeros(acc_scr.shape, F32)

    def gate_block(i_rel):
        w = jnp.zeros((nk, T), BF16)
        grp, ii = divmod(i_rel, group)
        for hd in range(PEER_HEADS):
            rs = slice(hd * nk, (hd + 1) * nk)
            base = pl.multiple_of(hd * nk + ec * (len(wa_scrs) * per) + grp * group, group)
            cnt_row = cnt_ref[pl.ds(base, group), :][ii:ii + 1, :]
            e1_row = e1_ref[pl.ds(base, group), :][ii:ii + 1, :]
            cb = jnp.broadcast_to(cnt_row, (nk, T)).astype(BF16)
            eb = jnp.broadcast_to(e1_row, (nk, T)).astype(BF16)
            sel = r2_ref[rs, :] < cb
            w = w + jnp.where(sel, e2_ref[rs, :], jnp.zeros((), BF16)) * eb
        return w

    total = None
    for sidx, wa_scr in enumerate(wa_scrs):
        for c in range(per // chunk):
            rows = slice(c * chunk * nk, (c + 1) * chunk * nk)
            urows = slice(sidx * sub_size + c * chunk * nk, sidx * sub_size + (c + 1) * chunk * nk)
            act = jnp.dot(u_ref[urows, :], ht_ref[...], preferred_element_type=F32)
            ab = act.astype(BF16)
            g = ab * (1.0 + lax.erf(ab * 0.7071067811865476))
            w = jnp.concatenate([gate_block(sidx * per + c * chunk + j) for j in range(chunk)],
                                axis=0)
            wa_scr[rows, :] = w * g
        y = jnp.dot(vt_ref[:, sidx * sub_size:(sidx + 1) * sub_size], wa_scr[...],
                    preferred_element_type=F32)
        total = y if total is None else total + y
    acc_scr[...] += total

    @pl.when(ec == n_steps - 1)
    def _():
        g2 = mod_ref[0, 5:6, :]
        y = x_ref[...] + g2 * acc_scr[...].T
        if final_norm:
            ms = jnp.mean(y * y, axis=-1, keepdims=True)
            y = y * lax.rsqrt(ms + EPS) * fn_ref[...]
        o_ref[...] = y


def _peer(x, mod, gn, w_q, sub_keys, u_bf, vt_bf, fn=None):
    B, S, D = x.shape
    N = B * S
    H, NK = PEER_HEADS, PEER_NKEYS
    E = u_bf.shape[0]
    tr = ROUTE_T
    x2 = x.reshape(N, D)
    wq = w_q.reshape(D, H * 2 * NK).astype(BF16)
    sk = sub_keys.reshape(H * 2, NK, NK).astype(BF16)
    ht, cnt, e1, r2, e2 = pl.pallas_call(
        _route_kernel,
        out_shape=(
            jax.ShapeDtypeStruct((D, N), BF16),
            jax.ShapeDtypeStruct((H * NK, N), F32),
            jax.ShapeDtypeStruct((H * NK, N), F32),
            jax.ShapeDtypeStruct((H * NK, N), BF16),
            jax.ShapeDtypeStruct((H * NK, N), BF16),
        ),
        grid=(N // tr,),
        in_specs=[
            pl.BlockSpec((tr, D), lambda i: (i, 0)),
            pl.BlockSpec((1, 6, D), lambda i: ((i * tr) // S, 0, 0)),
            pl.BlockSpec((1, D), lambda i: (0, 0)),
            pl.BlockSpec((D, H * 2 * NK), lambda i: (0, 0)),
            pl.BlockSpec((H * 2, NK, NK), lambda i: (0, 0, 0)),
        ],
        out_specs=(
            pl.BlockSpec((D, tr), lambda i: (0, i)),
            pl.BlockSpec((H * NK, tr), lambda i: (0, i)),
            pl.BlockSpec((H * NK, tr), lambda i: (0, i)),
            pl.BlockSpec((H * NK, tr), lambda i: (0, i)),
            pl.BlockSpec((H * NK, tr), lambda i: (0, i)),
        ),
        compiler_params=_params(("arbitrary",)),
        name="peer_route",
    )(x2, mod, gn.reshape(1, D), wq, sk)

    td, sub, nsub = DENSE_T, DENSE_SUB, DENSE_NSUB
    ec = sub * nsub
    assert (sub // NK) % 8 == 0 and (sub // NK) % DENSE_CHUNK == 0 and E % ec == 0
    assert S % td == 0
    rank_spec = pl.BlockSpec((H * NK, td), lambda t, e: (0, t))
    fn_arr = jnp.ones((1, D), F32) if fn is None else fn.reshape(1, D)
    out = pl.pallas_call(
        functools.partial(_dense_kernel, sub_size=sub, chunk=DENSE_CHUNK, n_steps=E // ec,
                          final_norm=fn is not None),
        out_shape=jax.ShapeDtypeStruct((N, D), F32),
        grid=(N // td, E // ec),
        in_specs=[
            pl.BlockSpec((D, td), lambda t, e: (0, t)),
            pl.BlockSpec((ec, D), lambda t, e: (e, 0)),
            pl.BlockSpec((D, ec), lambda t, e: (0, e)),
            rank_spec, rank_spec, rank_spec, rank_spec,
            pl.BlockSpec((td, D), lambda t, e: (t, 0)),
            pl.BlockSpec((1, 6, D), lambda t, e: ((t * td) // S, 0, 0)),
            pl.BlockSpec((1, D), lambda t, e: (0, 0)),
        ],
        out_specs=pl.BlockSpec((td, D), lambda t, e: (t, 0)),
        scratch_shapes=[pltpu.VMEM((D, td), F32)] + [pltpu.VMEM((sub, td), BF16)] * nsub,
        compiler_params=_params(("arbitrary", "arbitrary")),
        name="peer_experts",
    )(ht, u_bf, vt_bf, cnt, e1, r2, e2, x2, mod, fn_arr)
    return out.reshape(B, S, D)


def kernel(x, c, positions, ada_w, ada_b, norm_mix, norm_ffn, ev_w_in, ev_g_v, ev_w_s, ev_b_s,
           ev_w_pool, ev_pool_scale, ev_w_out, od_w_in, od_lam_q1, od_lam_k1, od_lam_q2,
           od_lam_k2, od_g_sub, od_w_out, peer_w_q, peer_sub_keys, peer_u, peer_v, final_norm):
    B, S, D = x.shape
    depth = ada_w.shape[0]
    inv_freq = 1.0 / (ROPE_THETA ** (jnp.arange(0, DA_QK, 2, dtype=F32) / DA_QK))
    ang = positions.astype(F32)[..., None] * inv_freq
    cos_t = jnp.swapaxes(jnp.cos(ang), 1, 2)
    sin_t = jnp.swapaxes(jnp.sin(ang), 1, 2)
    mods = _ada(c, ada_w, ada_b)
    for l in range(depth):
        mod = mods[l]
        if l % 2 == 0:
            e = l // 2
            x = _even_layer(x, mod, norm_mix[l], ev_w_in[e], ev_g_v[e], ev_w_s[e], ev_b_s[e],
                            ev_w_pool[e], ev_pool_scale[e], ev_w_out[e])
        else:
            o = l // 2
            lam_init = 0.8 - 0.6 * math.exp(-0.3 * l)
            qt, k, vt = _qkv(x, mod, norm_mix[l], od_w_in[o], cos_t, sin_t)
            lam_vecs = jnp.stack([od_lam_q1[o], od_lam_k1[o], od_lam_q2[o], od_lam_k2[o]])
            ot = _attention(qt, k, vt, lam_vecs, od_g_sub[o], lam_init)
            x = _oproj(ot, od_w_out[o], x, mod)
        x = _peer(x, mod, norm_ffn[l], peer_w_q[l], peer_sub_keys[l],
                  peer_u[l].astype(BF16), peer_v[l].T.astype(BF16),
                  fn=final_norm if l == depth - 1 else None)
    return x
```

```python
import functools
import math

import jax
import jax.numpy as jnp
from jax import lax
from jax.experimental import pallas as pl
from jax.experimental.pallas import tpu as pltpu

F32 = jnp.float32
BF16 = jnp.bfloat16
EPS = 1e-6
NEG_INF = float("-inf")

GM_GROUPS = 4
GM_CH = 128
GM_WIDTH = GM_GROUPS * GM_CH
GM_CHUNK = 128
POOL_WINDOWS = (2, 4, 8, 16)
POOL_CH = 128
POOL_WIDTH = len(POOL_WINDOWS) * POOL_CH
POOL_HALO = max(POOL_WINDOWS)
DA_HEADS = 8
DA_QK = 64
DA_V = 128
ROPE_THETA = 10000.0
PEER_HEADS = 8
PEER_NKEYS = 128
PEER_TOPK = 16

F32_SUBLANES = 8

V7X_VMEM_LIMIT = 56 * 1024 * 1024

EVEN_TS = 512
QKV_TS = 512
ATT_TQ = 512
ATT_TK = 512
ATT_TK_BIG = 1024
ROUTE_T = 256
DENSE_T = 512
DENSE_SUB = 2048
DENSE_NSUB = 1
DENSE_CHUNK = 2


def _params(sem):
    return pltpu.CompilerParams(dimension_semantics=sem, vmem_limit_bytes=V7X_VMEM_LIMIT)


def _gelu(x):
    return 0.5 * x * (1.0 + lax.erf(x * 0.7071067811865476))


def _norm_mod(x, gn, sc, sh):
    ms = jnp.mean(x * x, axis=-1, keepdims=True)
    return x * lax.rsqrt(ms + EPS) * gn * (1.0 + sc) + sh


def _ada_kernel(c_ref, w_ref, b_ref, o_ref):
    c = c_ref[...]
    ca = c * jax.nn.sigmoid(c)
    o_ref[0, 0] = jnp.dot(ca, w_ref[0], preferred_element_type=F32,
                          precision=lax.Precision.HIGHEST) + b_ref[0, 0]


def _ada(c, ada_w, ada_b):
    L, D, _ = ada_w.shape
    B = c.shape[0]
    out = pl.pallas_call(
        _ada_kernel,
        out_shape=jax.ShapeDtypeStruct((L, 6, B, D), F32),
        grid=(L, 6),
        in_specs=[
            pl.BlockSpec((B, D), lambda l, j: (0, 0)),
            pl.BlockSpec((1, D, D), lambda l, j: (l, 0, j)),
            pl.BlockSpec((1, 1, 1, D), lambda l, j: (l, j, 0, 0)),
        ],
        out_specs=pl.BlockSpec((1, 1, B, D), lambda l, j: (l, j, 0, 0)),
        compiler_params=_params(("arbitrary", "arbitrary")),
        name="ada",
    )(c, ada_w, ada_b.reshape(L, 6, 1, D))
    return jnp.swapaxes(out, 1, 2)


def _even_kernel(x_ref, mod_ref, gn_ref, win_ref, gv_ref, ws_ref, bs_ref, wp_ref, ls_ref,
                 wout_ref, o_ref, ext_ref, mix_ref, *, ts):
    si = pl.program_id(1)
    x = x_ref[0]
    sh1 = mod_ref[0, 0:1, :]
    sc1 = mod_ref[0, 1:2, :]
    g1 = mod_ref[0, 2:3, :]
    h = _norm_mod(x, gn_ref[...], sc1, sh1)
    proj = jnp.dot(h.astype(BF16), win_ref[...], preferred_element_type=F32)

    z = _gelu(proj[:, :2 * GM_WIDTH])
    u = z[:, :GM_WIDTH]
    v = z[:, GM_WIDTH:]
    mu = jnp.mean(v, axis=-1, keepdims=True)
    d = v - mu
    var = jnp.mean(d * d, axis=-1, keepdims=True)
    vn = (d * lax.rsqrt(var + EPS) * gv_ref[...]).astype(BF16)
    r = lax.broadcasted_iota(jnp.int32, (GM_CHUNK, GM_CHUNK), 0)
    cidx = lax.broadcasted_iota(jnp.int32, (GM_CHUNK, GM_CHUNK), 1)
    causal = r >= cidx
    for g in range(GM_GROUPS):
        wg = jnp.where(causal, ws_ref[g], 0.0).astype(BF16)
        for c in range(ts // GM_CHUNK):
            rs = slice(c * GM_CHUNK, (c + 1) * GM_CHUNK)
            cs = slice(g * GM_CH, (g + 1) * GM_CH)
            sv = jnp.dot(wg, vn[rs, cs], preferred_element_type=F32) + bs_ref[g]
            mix_ref[rs, cs] = (u[rs, cs] * sv).astype(BF16)

    p = proj[:, 2 * GM_WIDTH:]

    @pl.when(si == 0)
    def _():
        ext_ref[0:POOL_HALO, :] = jnp.zeros((POOL_HALO, POOL_WIDTH), F32)

    ext_ref[POOL_HALO:POOL_HALO + ts, :] = p
    t = si * ts + lax.broadcasted_iota(jnp.int32, (ts, POOL_CH), 0)
    for g, w in enumerate(POOL_WINDOWS):
        cs = slice(g * POOL_CH, (g + 1) * POOL_CH)
        acc = ext_ref[POOL_HALO:POOL_HALO + ts, cs]
        for k in range(1, w):
            acc = acc + ext_ref[POOL_HALO - k:POOL_HALO - k + ts, cs]
        cnt = jnp.minimum(t + 1, w).astype(F32)
        pooled = acc / cnt - p[:, cs]
        yb = jnp.dot(pooled.astype(BF16), wp_ref[g], preferred_element_type=F32) * ls_ref[:, cs]
        mix_ref[:, GM_WIDTH + g * POOL_CH:GM_WIDTH + (g + 1) * POOL_CH] = yb.astype(BF16)
    ext_ref[0:POOL_HALO, :] = ext_ref[ts:ts + POOL_HALO, :]

    y = jnp.dot(mix_ref[...], wout_ref[...], preferred_element_type=F32)
    o_ref[0] = x + g1 * y


def _even_layer(x, mod, gn, w_in, g_v, w_s, b_s, w_pool, ls, w_out):
    B, S, D = x.shape
    ts = EVEN_TS
    n_in = w_in.shape[1]
    bsb = jnp.broadcast_to(b_s[:, :, None], (GM_GROUPS, GM_CHUNK, GM_CH))
    return pl.pallas_call(
        functools.partial(_even_kernel, ts=ts),
        out_shape=jax.ShapeDtypeStruct((B, S, D), F32),
        grid=(B, S // ts),
        in_specs=[
            pl.BlockSpec((1, ts, D), lambda b, s: (b, s, 0)),
            pl.BlockSpec((1, 6, D), lambda b, s: (b, 0, 0)),
            pl.BlockSpec((1, D), lambda b, s: (0, 0)),
            pl.BlockSpec((D, n_in), lambda b, s: (0, 0)),
            pl.BlockSpec((1, GM_WIDTH), lambda b, s: (0, 0)),
            pl.BlockSpec((GM_GROUPS, GM_CHUNK, GM_CHUNK), lambda b, s: (0, 0, 0)),
            pl.BlockSpec((GM_GROUPS, GM_CHUNK, GM_CH), lambda b, s: (0, 0, 0)),
            pl.BlockSpec((len(POOL_WINDOWS), POOL_CH, POOL_CH), lambda b, s: (0, 0, 0)),
            pl.BlockSpec((1, POOL_WIDTH), lambda b, s: (0, 0)),
            pl.BlockSpec((GM_WIDTH + POOL_WIDTH, D), lambda b, s: (0, 0)),
        ],
        out_specs=pl.BlockSpec((1, ts, D), lambda b, s: (b, s, 0)),
        scratch_shapes=[
            pltpu.VMEM((POOL_HALO + ts, POOL_WIDTH), F32),
            pltpu.VMEM((ts, GM_WIDTH + POOL_WIDTH), BF16),
        ],
        compiler_params=_params(("arbitrary", "arbitrary")),
        name="even_mixer",
    )(x, mod, gn.reshape(1, D), w_in.astype(BF16), g_v.reshape(1, -1), w_s, bsb,
      w_pool.astype(BF16), ls.reshape(1, -1), w_out.astype(BF16))


def _qkv_kernel(x_ref, mod_ref, gn_ref, wt_ref, cos_ref, sin_ref, qt_ref, k_ref, vt_ref,
                kt_scr, *, d_model):
    x = x_ref[0]
    sh1 = mod_ref[0, 0:1, :]
    sc1 = mod_ref[0, 1:2, :]
    h = _norm_mod(x, gn_ref[...], sc1, sh1)
    ht = h.T.astype(BF16)
    pt = jnp.dot(wt_ref[...], ht, preferred_element_type=F32)
    cos = cos_ref[0]
    sin = sin_ref[0]
    half = DA_QK // 2
    scale = DA_QK ** -0.5 * math.log2(math.e)
    for g in range(2 * DA_HEADS):
        r0 = g * DA_QK
        t1 = pt[r0:r0 + half]
        t2 = pt[r0 + half:r0 + DA_QK]
        qt_ref[0, r0:r0 + half, :] = ((t1 * cos - t2 * sin) * scale).astype(BF16)
        qt_ref[0, r0 + half:r0 + DA_QK, :] = ((t2 * cos + t1 * sin) * scale).astype(BF16)
        k0 = d_model + r0
        t1 = pt[k0:k0 + half]
        t2 = pt[k0 + half:k0 + DA_QK]
        kt_scr[r0:r0 + half, :] = t1 * cos - t2 * sin
        kt_scr[r0 + half:r0 + DA_QK, :] = t2 * cos + t1 * sin
    k_ref[0] = kt_scr[...].T.astype(BF16)
    vt_ref[0] = pt[2 * d_model:3 * d_model].astype(BF16)


def _qkv(x, mod, gn, w_in, cos_t, sin_t):
    B, S, D = x.shape
    ts = QKV_TS
    wt = w_in.T.astype(BF16)
    return pl.pallas_call(
        functools.partial(_qkv_kernel, d_model=D),
        out_shape=(
            jax.ShapeDtypeStruct((B, D, S), BF16),
            jax.ShapeDtypeStruct((B, S, D), BF16),
            jax.ShapeDtypeStruct((B, D, S), BF16),
        ),
        grid=(B, S // ts),
        in_specs=[
            pl.BlockSpec((1, ts, D), lambda b, s: (b, s, 0)),
            pl.BlockSpec((1, 6, D), lambda b, s: (b, 0, 0)),
            pl.BlockSpec((1, D), lambda b, s: (0, 0)),
            pl.BlockSpec((3 * D, D), lambda b, s: (0, 0)),
            pl.BlockSpec((1, DA_QK // 2, ts), lambda b, s: (b, 0, s)),
            pl.BlockSpec((1, DA_QK // 2, ts), lambda b, s: (b, 0, s)),
        ],
        out_specs=(
            pl.BlockSpec((1, D, ts), lambda b, s: (b, 0, s)),
            pl.BlockSpec((1, ts, D), lambda b, s: (b, s, 0)),
            pl.BlockSpec((1, D, ts), lambda b, s: (b, 0, s)),
        ),
        scratch_shapes=[pltpu.VMEM((D, ts), F32)],
        compiler_params=_params(("arbitrary", "arbitrary")),
        name="qkv_rope",
    )(x, mod, gn.reshape(1, D), wt, cos_t, sin_t)


def _attn_kernel(qt_ref, k_ref, vt_ref, lam_ref, gs_ref, o_ref, *, tq, tk, tk_big, lam_init):
    qi = pl.program_id(2)
    qt = qt_ref[0]
    row = lax.broadcasted_iota(jnp.int32, qt.shape, 0)
    zero = jnp.zeros_like(qt)
    q12 = jnp.concatenate([jnp.where(row < DA_QK, qt, zero), jnp.where(row >= DA_QK, qt, zero)],
                          axis=1)

    def update(start, size, masked, carry):
        m_old, l_old, acc_old = carry
        off = pl.multiple_of(start, tk)
        kb = k_ref[0, pl.ds(off, size), :]
        vb = vt_ref[0, :, pl.ds(off, size)]
        s = jnp.dot(kb, q12, preferred_element_type=F32)
        if masked:
            kpos = off + lax.broadcasted_iota(jnp.int32, s.shape, 0)
            lane = lax.broadcasted_iota(jnp.int32, s.shape, 1)
            qpos = qi * tq + jnp.where(lane >= tq, lane - tq, lane)
            s = jnp.where(kpos <= qpos, s, NEG_INF)
        m_new = jnp.maximum(m_old, jnp.max(s, axis=0, keepdims=True))
        alpha = jnp.exp2(m_old - m_new)
        p = jnp.exp2(s - m_new)
        l_new = alpha * l_old + jnp.sum(p, axis=0, keepdims=True)
        acc_new = alpha * acc_old + jnp.dot(vb, p.astype(BF16), preferred_element_type=F32)
        return m_new, l_new, acc_new

    init = (jnp.full((1, 2 * tq), NEG_INF, F32), jnp.zeros((1, 2 * tq), F32),
            jnp.zeros((DA_V, 2 * tq), F32))
    visible = qi * tq
    n_big = visible // tk_big
    carry = lax.fori_loop(0, n_big, lambda j, c: update(j * tk_big, tk_big, False, c), init)
    carry = lax.cond(visible - n_big * tk_big >= tk,
                     lambda c: update(n_big * tk_big, tk, False, c), lambda c: c, carry)
    _, l, acc = update(visible, tk, True, carry)

    lv = lam_ref[...]
    la = jnp.sum(lv[0:1] * lv[1:2], axis=-1, keepdims=True)
    lb = jnp.sum(lv[2:3] * lv[3:4], axis=-1, keepdims=True)
    lam = jnp.exp(la) - jnp.exp(lb) + lam_init
    o = acc[:, :tq] / l[:, :tq] - lam * (acc[:, tq:] / l[:, tq:])
    ms = jnp.mean(o * o, axis=0, keepdims=True)
    o = o * lax.rsqrt(ms + EPS) * gs_ref[...] * (1.0 - lam_init)
    o_ref[0] = o.astype(BF16)


def _attention(qt, k, vt, lam_vecs, g_sub, lam_init):
    B, D, S = qt.shape
    tq, tk, tk_big = ATT_TQ, ATT_TK, ATT_TK_BIG
    assert tk == tq and tk_big == 2 * tk and S % tk == 0
    hd = 2 * DA_QK
    return pl.pallas_call(
        functools.partial(_attn_kernel, tq=tq, tk=tk, tk_big=tk_big, lam_init=lam_init),
        out_shape=jax.ShapeDtypeStruct((B, D, S), BF16),
        grid=(B, DA_HEADS, S // tq),
        in_specs=[
            pl.BlockSpec((1, hd, tq), lambda b, h, q: (b, h, q)),
            pl.BlockSpec((1, S, hd), lambda b, h, q: (b, 0, h)),
            pl.BlockSpec((1, DA_V, S), lambda b, h, q: (b, h, 0)),
            pl.BlockSpec((4, DA_QK), lambda b, h, q: (0, 0)),
            pl.BlockSpec((DA_V, 1), lambda b, h, q: (0, 0)),
        ],
        out_specs=pl.BlockSpec((1, DA_V, tq), lambda b, h, q: (b, h, q)),
        compiler_params=_params(("arbitrary", "arbitrary", "arbitrary")),
        name="diff_attention",
    )(qt, k, vt, lam_vecs, g_sub.reshape(DA_V, 1))


def _oproj_kernel(ot_ref, wt_ref, x_ref, mod_ref, o_ref):
    yt = jnp.dot(wt_ref[...], ot_ref[0], preferred_element_type=F32)
    g1 = mod_ref[0, 2:3, :]
    o_ref[0] = x_ref[0] + g1 * yt.T


def _oproj(ot, w_out, x, mod):
    B, S, D = x.shape
    ts = QKV_TS
    return pl.pallas_call(
        _oproj_kernel,
        out_shape=jax.ShapeDtypeStruct((B, S, D), F32),
        grid=(B, S // ts),
        in_specs=[
            pl.BlockSpec((1, D, ts), lambda b, s: (b, 0, s)),
            pl.BlockSpec((D, D), lambda b, s: (0, 0)),
            pl.BlockSpec((1, ts, D), lambda b, s: (b, s, 0)),
            pl.BlockSpec((1, 6, D), lambda b, s: (b, 0, 0)),
        ],
        out_specs=pl.BlockSpec((1, ts, D), lambda b, s: (b, s, 0)),
        compiler_params=_params(("arbitrary", "arbitrary")),
        name="attn_out_proj",
    )(ot, w_out.T.astype(BF16), x, mod)


def _top_pair_sums(a, b, k):
    n = F32_SUBLANES
    assert k == 2 * n and len(a) == k and len(b) == k
    b_lo = jnp.concatenate(b[:n], axis=0)
    lists = [a[p] + b_lo for p in range(n)]
    row0 = a[0] + jnp.concatenate(b[n:], axis=0)
    col0 = jnp.concatenate(a[n:], axis=0) + b[0]
    vals = []
    for _ in range(k):
        head = jnp.maximum(jnp.maximum(lists[0], row0), col0)
        m = jnp.max(head, axis=0, keepdims=True)
        vals.append(m)
        hit = lists[0] == m
        for r in range(n - 1):
            lists[r] = jnp.where(hit, lists[r + 1], lists[r])
        lists[n - 1] = jnp.where(hit, NEG_INF, lists[n - 1])
        row0 = jnp.where(row0 == m, NEG_INF, row0)
        col0 = jnp.where(col0 == m, NEG_INF, col0)
    return vals


def _rank_among(s, b):
    assert len(b) == 16
    lo8 = s >= b[7]
    lo4 = s >= jnp.where(lo8, b[3], b[11])
    lo2 = s >= jnp.where(lo8, jnp.where(lo4, b[1], b[5]), jnp.where(lo4, b[9], b[13]))
    lo1 = s >= jnp.where(
        lo8,
        jnp.where(lo4, jnp.where(lo2, b[0], b[2]), jnp.where(lo2, b[4], b[6])),
        jnp.where(lo4, jnp.where(lo2, b[8], b[10]), jnp.where(lo2, b[12], b[14])))
    rank = (jnp.where(lo8, 0.0, 8.0) + jnp.where(lo4, 0.0, 4.0)
            + jnp.where(lo2, 0.0, 2.0) + jnp.where(lo1, 0.0, 1.0))
    return jnp.where(s >= b[15], rank, 16.0)


def _top_values_128(s, k):
    n = s.shape[0] // 8
    assert n == 16 and k <= n
    blocks = [s[8 * v:8 * v + 8, :] for v in range(n)]
    size = 2
    while size <= n:
        stride = size // 2
        while stride >= 1:
            for i in range(n):
                j = i ^ stride
                if j > i:
                    hi = jnp.maximum(blocks[i], blocks[j])
                    lo = jnp.minimum(blocks[i], blocks[j])
                    blocks[i], blocks[j] = (hi, lo) if (i & size) == 0 else (lo, hi)
            stride //= 2
        size *= 2
    vals = []
    for t in range(k):
        head = blocks[0]
        m = jnp.max(head, axis=0, keepdims=True)
        vals.append(m)
        hit = head == m
        for r in range(k - 1 - t):
            blocks[r] = jnp.where(hit, blocks[r + 1], blocks[r])
    return vals


def _route_kernel(x_ref, mod_ref, gn_ref, wq_ref, sk_ref, ht_ref, cnt_ref, e1_ref, r2_ref, e2_ref):
    x = x_ref[...]
    sh2 = mod_ref[0, 3:4, :]
    sc2 = mod_ref[0, 4:5, :]
    h = _norm_mod(x, gn_ref[...], sc2, sh2)
    ht_ref[...] = h.T.astype(BF16)
    q = jnp.dot(h.astype(BF16), wq_ref[...], preferred_element_type=F32)
    K = PEER_TOPK
    nk = PEER_NKEYS
    for hd in range(PEER_HEADS):
        rs = slice(hd * nk, (hd + 1) * nk)
        s12 = []
        for p in range(2):
            c0 = (hd * 2 + p) * nk
            qhp = q[:, c0:c0 + nk].astype(BF16)
            s12.append(lax.dot_general(sk_ref[hd * 2 + p], qhp, (((1,), (1,)), ((), ())),
                                       preferred_element_type=F32))
        s1, s2 = s12
        a = _top_values_128(s1, K)
        b = _top_values_128(s2, K)
        r2 = _rank_among(s2, b)
        c = _top_pair_sums(a, b, K)
        tau = c[K - 1]
        zsum = jnp.ones_like(c[0])
        for k in range(1, K):
            zsum = zsum + jnp.exp(c[k] - c[0])
        a_all = jnp.concatenate(a, axis=0)
        ck = jnp.zeros(a_all.shape, F32)
        for l in range(K):
            ck = ck + jnp.where((a_all + b[l]) >= tau, 1.0, 0.0)
        cnt = jnp.zeros(s1.shape, F32)
        for k in range(K):
            cnt = jnp.where(s1 == a[k], ck[k:k + 1, :], cnt)
        cnt_ref[rs, :] = cnt
        e1_ref[rs, :] = jnp.exp(s1 - a[0])
        r2_ref[rs, :] = r2.astype(BF16)
        e2_ref[rs, :] = (0.5 * jnp.exp(s2 - b[0]) / zsum).astype(BF16)


def _dense_kernel(ht_ref, u_ref, vt_ref, cnt_ref, e1_ref, r2_ref, e2_ref, x_ref, mod_ref, fn_ref,
                  o_ref, acc_scr, *wa_scrs, sub_size, chunk, n_steps, final_norm):
    ec = pl.program_id(1)
    nk = PEER_NKEYS
    per = sub_size // nk
    group = F32_SUBLANES
    T = ht_ref.shape[1]

    @pl.when(ec == 0)
    def _():
        acc_scr[...] = jnp.zeros(acc_scr.shape, F32)

    def gate_block(i_rel):
        w = jnp.zeros((nk, T), BF16)
        grp, ii = divmod(i_rel, group)
        for hd in range(PEER_HEADS):
            rs = slice(hd * nk, (hd + 1) * nk)
            base = pl.multiple_of(hd * nk + ec * (len(wa_scrs) * per) + grp * group, group)
            cnt_row = cnt_ref[pl.ds(base, group), :][ii:ii + 1, :]
            e1_row = e1_ref[pl.ds(base, group), :][ii:ii + 1, :]
            cb = jnp.broadcast_to(cnt_row, (nk, T)).astype(BF16)
            eb = jnp.broadcast_to(e1_row, (nk, T)).astype(BF16)
            sel = r2_ref[rs, :] < cb
            w = w + jnp.where(sel, e2_ref[rs, :], jnp.zeros((), BF16)) * eb
        return w

    total = None
    for sidx, wa_scr in enumerate(wa_scrs):
        for c in range(per // chunk):
            rows = slice(c * chunk * nk, (c + 1) * chunk * nk)
            urows = slice(sidx * sub_size + c * chunk * nk, sidx * sub_size + (c + 1) * chunk * nk)
            act = jnp.dot(u_ref[urows, :], ht_ref[...], preferred_element_type=F32)
            ab = act.astype(BF16)
            g = ab * (1.0 + lax.erf(ab * 0.7071067811865476))
            w = jnp.concatenate([gate_block(sidx * per + c * chunk + j) for j in range(chunk)],
                                axis=0)
            wa_scr[rows, :] = w * g
        y = jnp.dot(vt_ref[:, sidx * sub_size:(sidx + 1) * sub_size], wa_scr[...],
                    preferred_element_type=F32)
        total = y if total is None else total + y
    acc_scr[...] += total

    @pl.when(ec == n_steps - 1)
    def _():
        g2 = mod_ref[0, 5:6, :]
        y = x_ref[...] + g2 * acc_scr[...].T
        if final_norm:
            ms = jnp.mean(y * y, axis=-1, keepdims=True)
            y = y * lax.rsqrt(ms + EPS) * fn_ref[...]
        o_ref[...] = y


def _peer(x, mod, gn, w_q, sub_keys, u_bf, vt_bf, fn=None):
    B, S, D = x.shape
    N = B * S
    H, NK = PEER_HEADS, PEER_NKEYS
    E = u_bf.shape[0]
    tr = ROUTE_T
    x2 = x.reshape(N, D)
    wq = w_q.reshape(D, H * 2 * NK).astype(BF16)
    sk = sub_keys.reshape(H * 2, NK, NK).astype(BF16)
    ht, cnt, e1, r2, e2 = pl.pallas_call(
        _route_kernel,
        out_shape=(
            jax.ShapeDtypeStruct((D, N), BF16),
            jax.ShapeDtypeStruct((H * NK, N), F32),
            jax.ShapeDtypeStruct((H * NK, N), F32),
            jax.ShapeDtypeStruct((H * NK, N), BF16),
            jax.ShapeDtypeStruct((H * NK, N), BF16),
        ),
        grid=(N // tr,),
        in_specs=[
            pl.BlockSpec((tr, D), lambda i: (i, 0)),
            pl.BlockSpec((1, 6, D), lambda i: ((i * tr) // S, 0, 0)),
            pl.BlockSpec((1, D), lambda i: (0, 0)),
            pl.BlockSpec((D, H * 2 * NK), lambda i: (0, 0)),
            pl.BlockSpec((H * 2, NK, NK), lambda i: (0, 0, 0)),
        ],
        out_specs=(
            pl.BlockSpec((D, tr), lambda i: (0, i)),
            pl.BlockSpec((H * NK, tr), lambda i: (0, i)),
            pl.BlockSpec((H * NK, tr), lambda i: (0, i)),
            pl.BlockSpec((H * NK, tr), lambda i: (0, i)),
            pl.BlockSpec((H * NK, tr), lambda i: (0, i)),
        ),
        compiler_params=_params(("arbitrary",)),
        name="peer_route",
    )(x2, mod, gn.reshape(1, D), wq, sk)

    td, sub, nsub = DENSE_T, DENSE_SUB, DENSE_NSUB
    ec = sub * nsub
    assert (sub // NK) % 8 == 0 and (sub // NK) % DENSE_CHUNK == 0 and E % ec == 0
    assert S % td == 0
    rank_spec = pl.BlockSpec((H * NK, td), lambda t, e: (0, t))
    fn_arr = jnp.ones((1, D), F32) if fn is None else fn.reshape(1, D)
    out = pl.pallas_call(
        functools.partial(_dense_kernel, sub_size=sub, chunk=DENSE_CHUNK, n_steps=E // ec,
                          final_norm=fn is not None),
        out_shape=jax.ShapeDtypeStruct((N, D), F32),
        grid=(N // td, E // ec),
        in_specs=[
            pl.BlockSpec((D, td), lambda t, e: (0, t)),
            pl.BlockSpec((ec, D), lambda t, e: (e, 0)),
            pl.BlockSpec((D, ec), lambda t, e: (0, e)),
            rank_spec, rank_spec, rank_spec, rank_spec,
            pl.BlockSpec((td, D), lambda t, e: (t, 0)),
            pl.BlockSpec((1, 6, D), lambda t, e: ((t * td) // S, 0, 0)),
            pl.BlockSpec((1, D), lambda t, e: (0, 0)),
        ],
        out_specs=pl.BlockSpec((td, D), lambda t, e: (t, 0)),
        scratch_shapes=[pltpu.VMEM((D, td), F32)] + [pltpu.VMEM((sub, td), BF16)] * nsub,
        compiler_params=_params(("arbitrary", "arbitrary")),
        name="peer_experts",
    )(ht, u_bf, vt_bf, cnt, e1, r2, e2, x2, mod, fn_arr)
    return out.reshape(B, S, D)


def kernel(x, c, positions, ada_w, ada_b, norm_mix, norm_ffn, ev_w_in, ev_g_v, ev_w_s, ev_b_s,
           ev_w_pool, ev_pool_scale, ev_w_out, od_w_in, od_lam_q1, od_lam_k1, od_lam_q2,
           od_lam_k2, od_g_sub, od_w_out, peer_w_q, peer_sub_keys, peer_u, peer_v, final_norm):
    B, S, D = x.shape
    depth = ada_w.shape[0]
    inv_freq = 1.0 / (ROPE_THETA ** (jnp.arange(0, DA_QK, 2, dtype=F32) / DA_QK))
    ang = positions.astype(F32)[..., None] * inv_freq
    cos_t = jnp.swapaxes(jnp.cos(ang), 1, 2)
    sin_t = jnp.swapaxes(jnp.sin(ang), 1, 2)
    mods = _ada(c, ada_w, ada_b)
    for l in range(depth):
        mod = mods[l]
        if l % 2 == 0:
            e = l // 2
            x = _even_layer(x, mod, norm_mix[l], ev_w_in[e], ev_g_v[e], ev_w_s[e], ev_b_s[e],
                            ev_w_pool[e], ev_pool_scale[e], ev_w_out[e])
        else:
            o = l // 2
            lam_init = 0.8 - 0.6 * math.exp(-0.3 * l)
            qt, k, vt = _qkv(x, mod, norm_mix[l], od_w_in[o], cos_t, sin_t)
            lam_vecs = jnp.stack([od_lam_q1[o], od_lam_k1[o], od_lam_q2[o], od_lam_k2[o]])
            ot = _attention(qt, k, vt, lam_vecs, od_g_sub[o], lam_init)
            x = _oproj(ot, od_w_out[o], x, mod)
        x = _peer(x, mod, norm_ffn[l], peer_w_q[l], peer_sub_keys[l],
                  peer_u[l].astype(BF16), peer_v[l].T.astype(BF16),
                  fn=final_norm if l == depth - 1 else None)
    return x
```

```python
import functools
import math

import jax
import jax.numpy as jnp
from jax import lax
from jax.experimental import pallas as pl
from jax.experimental.pallas import tpu as pltpu

F32 = jnp.float32
BF16 = jnp.bfloat16
EPS = 1e-6
NEG_INF = float("-inf")

GM_GROUPS = 4
GM_CH = 128
GM_WIDTH = GM_GROUPS * GM_CH
GM_CHUNK = 128
POOL_WINDOWS = (2, 4, 8, 16)
POOL_CH = 128
POOL_WIDTH = len(POOL_WINDOWS) * POOL_CH
POOL_HALO = max(POOL_WINDOWS)
DA_HEADS = 8
DA_QK = 64
DA_V = 128
ROPE_THETA = 10000.0
PEER_HEADS = 8
PEER_NKEYS = 128
PEER_TOPK = 16

F32_SUBLANES = 8

V7X_VMEM_LIMIT = 56 * 1024 * 1024

EVEN_TS = 512
QKV_TS = 512
ATT_TQ = 512
ATT_TK = 512
ATT_TK_BIG = 1024
ROUTE_T = 256
DENSE_T = 512
DENSE_SUB = 2048
DENSE_NSUB = 1
DENSE_CHUNK = 4


def _params(sem):
    return pltpu.CompilerParams(dimension_semantics=sem, vmem_limit_bytes=V7X_VMEM_LIMIT)


def _gelu(x):
    return 0.5 * x * (1.0 + lax.erf(x * 0.7071067811865476))


def _norm_mod(x, gn, sc, sh):
    ms = jnp.mean(x * x, axis=-1, keepdims=True)
    return x * lax.rsqrt(ms + EPS) * gn * (1.0 + sc) + sh


def _ada_kernel(c_ref, w_ref, b_ref, o_ref):
    c = c_ref[...]
    ca = c * jax.nn.sigmoid(c)
    o_ref[0, 0] = jnp.dot(ca, w_ref[0], preferred_element_type=F32,
                          precision=lax.Precision.HIGHEST) + b_ref[0, 0]


def _ada(c, ada_w, ada_b):
    L, D, _ = ada_w.shape
    B = c.shape[0]
    out = pl.pallas_call(
        _ada_kernel,
        out_shape=jax.ShapeDtypeStruct((L, 6, B, D), F32),
        grid=(L, 6),
        in_specs=[
            pl.BlockSpec((B, D), lambda l, j: (0, 0)),
            pl.BlockSpec((1, D, D), lambda l, j: (l, 0, j)),
            pl.BlockSpec((1, 1, 1, D), lambda l, j: (l, j, 0, 0)),
        ],
        out_specs=pl.BlockSpec((1, 1, B, D), lambda l, j: (l, j, 0, 0)),
        compiler_params=_params(("arbitrary", "arbitrary")),
        name="ada",
    )(c, ada_w, ada_b.reshape(L, 6, 1, D))
    return jnp.swapaxes(out, 1, 2)


def _even_kernel(x_ref, mod_ref, gn_ref, win_ref, gv_ref, ws_ref, bs_ref, wp_ref, ls_ref,
                 wout_ref, o_ref, ext_ref, mix_ref, *, ts):
    si = pl.program_id(1)
    x = x_ref[0]
    sh1 = mod_ref[0, 0:1, :]
    sc1 = mod_ref[0, 1:2, :]
    g1 = mod_ref[0, 2:3, :]
    h = _norm_mod(x, gn_ref[...], sc1, sh1)
    proj = jnp.dot(h.astype(BF16), win_ref[...], preferred_element_type=F32)

    z = _gelu(proj[:, :2 * GM_WIDTH])
    u = z[:, :GM_WIDTH]
    v = z[:, GM_WIDTH:]
    mu = jnp.mean(v, axis=-1, keepdims=True)
    d = v - mu
    var = jnp.mean(d * d, axis=-1, keepdims=True)
    vn = (d * lax.rsqrt(var + EPS) * gv_ref[...]).astype(BF16)
    r = lax.broadcasted_iota(jnp.int32, (GM_CHUNK, GM_CHUNK), 0)
    cidx = lax.broadcasted_iota(jnp.int32, (GM_CHUNK, GM_CHUNK), 1)
    causal = r >= cidx
    for g in range(GM_GROUPS):
        wg = jnp.where(causal, ws_ref[g], 0.0).astype(BF16)
        for c in range(ts // GM_CHUNK):
            rs = slice(c * GM_CHUNK, (c + 1) * GM_CHUNK)
            cs = slice(g * GM_CH, (g + 1) * GM_CH)
            sv = jnp.dot(wg, vn[rs, cs], preferred_element_type=F32) + bs_ref[g]
            mix_ref[rs, cs] = (u[rs, cs] * sv).astype(BF16)

    p = proj[:, 2 * GM_WIDTH:]

    @pl.when(si == 0)
    def _():
        ext_ref[0:POOL_HALO, :] = jnp.zeros((POOL_HALO, POOL_WIDTH), F32)

    ext_ref[POOL_HALO:POOL_HALO + ts, :] = p
    t = si * ts + lax.broadcasted_iota(jnp.int32, (ts, POOL_CH), 0)
    for g, w in enumerate(POOL_WINDOWS):
        cs = slice(g * POOL_CH, (g + 1) * POOL_CH)
        acc = ext_ref[POOL_HALO:POOL_HALO + ts, cs]
        for k in range(1, w):
            acc = acc + ext_ref[POOL_HALO - k:POOL_HALO - k + ts, cs]
        cnt = jnp.minimum(t + 1, w).astype(F32)
        pooled = acc / cnt - p[:, cs]
        yb = jnp.dot(pooled.astype(BF16), wp_ref[g], preferred_element_type=F32) * ls_ref[:, cs]
        mix_ref[:, GM_WIDTH + g * POOL_CH:GM_WIDTH + (g + 1) * POOL_CH] = yb.astype(BF16)
    ext_ref[0:POOL_HALO, :] = ext_ref[ts:ts + POOL_HALO, :]

    y = jnp.dot(mix_ref[...], wout_ref[...], preferred_element_type=F32)
    o_ref[0] = x + g1 * y


def _even_layer(x, mod, gn, w_in, g_v, w_s, b_s, w_pool, ls, w_out):
    B, S, D = x.shape
    ts = EVEN_TS
    n_in = w_in.shape[1]
    bsb = jnp.broadcast_to(b_s[:, :, None], (GM_GROUPS, GM_CHUNK, GM_CH))
    return pl.pallas_call(
        functools.partial(_even_kernel, ts=ts),
        out_shape=jax.ShapeDtypeStruct((B, S, D), F32),
        grid=(B, S // ts),
        in_specs=[
            pl.BlockSpec((1, ts, D), lambda b, s: (b, s, 0)),
            pl.BlockSpec((1, 6, D), lambda b, s: (b, 0, 0)),
            pl.BlockSpec((1, D), lambda b, s: (0, 0)),
            pl.BlockSpec((D, n_in), lambda b, s: (0, 0)),
            pl.BlockSpec((1, GM_WIDTH), lambda b, s: (0, 0)),
            pl.BlockSpec((GM_GROUPS, GM_CHUNK, GM_CHUNK), lambda b, s: (0, 0, 0)),
            pl.BlockSpec((GM_GROUPS, GM_CHUNK, GM_CH), lambda b, s: (0, 0, 0)),
            pl.BlockSpec((len(POOL_WINDOWS), POOL_CH, POOL_CH), lambda b, s: (0, 0, 0)),
            pl.BlockSpec((1, POOL_WIDTH), lambda b, s: (0, 0)),
            pl.BlockSpec((GM_WIDTH + POOL_WIDTH, D), lambda b, s: (0, 0)),
        ],
        out_specs=pl.BlockSpec((1, ts, D), lambda b, s: (b, s, 0)),
        scratch_shapes=[
            pltpu.VMEM((POOL_HALO + ts, POOL_WIDTH), F32),
            pltpu.VMEM((ts, GM_WIDTH + POOL_WIDTH), BF16),
        ],
        compiler_params=_params(("arbitrary", "arbitrary")),
        name="even_mixer",
    )(x, mod, gn.reshape(1, D), w_in.astype(BF16), g_v.reshape(1, -1), w_s, bsb,
      w_pool.astype(BF16), ls.reshape(1, -1), w_out.astype(BF16))


def _qkv_kernel(x_ref, mod_ref, gn_ref, wt_ref, cos_ref, sin_ref, qt_ref, k_ref, vt_ref,
                kt_scr, *, d_model):
    x = x_ref[0]
    sh1 = mod_ref[0, 0:1, :]
    sc1 = mod_ref[0, 1:2, :]
    h = _norm_mod(x, gn_ref[...], sc1, sh1)
    ht = h.T.astype(BF16)
    pt = jnp.dot(wt_ref[...], ht, preferred_element_type=F32)
    cos = cos_ref[0]
    sin = sin_ref[0]
    half = DA_QK // 2
    scale = DA_QK ** -0.5 * math.log2(math.e)
    for g in range(2 * DA_HEADS):
        r0 = g * DA_QK
        t1 = pt[r0:r0 + half]
        t2 = pt[r0 + half:r0 + DA_QK]
        qt_ref[0, r0:r0 + half, :] = ((t1 * cos - t2 * sin) * scale).astype(BF16)
        qt_ref[0, r0 + half:r0 + DA_QK, :] = ((t2 * cos + t1 * sin) * scale).astype(BF16)
        k0 = d_model + r0
        t1 = pt[k0:k0 + half]
        t2 = pt[k0 + half:k0 + DA_QK]
        kt_scr[r0:r0 + half, :] = t1 * cos - t2 * sin
        kt_scr[r0 + half:r0 + DA_QK, :] = t2 * cos + t1 * sin
    k_ref[0] = kt_scr[...].T.astype(BF16)
    vt_ref[0] = pt[2 * d_model:3 * d_model].astype(BF16)


def _qkv(x, mod, gn, w_in, cos_t, sin_t):
    B, S, D = x.shape
    ts = QKV_TS
    wt = w_in.T.astype(BF16)
    return pl.pallas_call(
        functools.partial(_qkv_kernel, d_model=D),
        out_shape=(
            jax.ShapeDtypeStruct((B, D, S), BF16),
            jax.ShapeDtypeStruct((B, S, D), BF16),
            jax.ShapeDtypeStruct((B, D, S), BF16),
        ),
        grid=(B, S // ts),
        in_specs=[
            pl.BlockSpec((1, ts, D), lambda b, s: (b, s, 0)),
            pl.BlockSpec((1, 6, D), lambda b, s: (b, 0, 0)),
            pl.BlockSpec((1, D), lambda b, s: (0, 0)),
            pl.BlockSpec((3 * D, D), lambda b, s: (0, 0)),
            pl.BlockSpec((1, DA_QK // 2, ts), lambda b, s: (b, 0, s)),
            pl.BlockSpec((1, DA_QK // 2, ts), lambda b, s: (b, 0, s)),
        ],
        out_specs=(
            pl.BlockSpec((1, D, ts), lambda b, s: (b, 0, s)),
            pl.BlockSpec((1, ts, D), lambda b, s: (b, s, 0)),
            pl.BlockSpec((1, D, ts), lambda b, s: (b, 0, s)),
        ),
        scratch_shapes=[pltpu.VMEM((D, ts), F32)],
        compiler_params=_params(("arbitrary", "arbitrary")),
        name="qkv_rope",
    )(x, mod, gn.reshape(1, D), wt, cos_t, sin_t)


def _attn_kernel(qt_ref, k_ref, vt_ref, lam_ref, gs_ref, o_ref, *, tq, tk, tk_big, lam_init):
    qi = pl.program_id(2)
    qt = qt_ref[0]
    row = lax.broadcasted_iota(jnp.int32, qt.shape, 0)
    zero = jnp.zeros_like(qt)
    q12 = jnp.concatenate([jnp.where(row < DA_QK, qt, zero), jnp.where(row >= DA_QK, qt, zero)],
                          axis=1)

    def update(start, size, masked, carry):
        m_old, l_old, acc_old = carry
        off = pl.multiple_of(start, tk)
        kb = k_ref[0, pl.ds(off, size), :]
        vb = vt_ref[0, :, pl.ds(off, size)]
        s = jnp.dot(kb, q12, preferred_element_type=F32)
        if masked:
            kpos = off + lax.broadcasted_iota(jnp.int32, s.shape, 0)
            lane = lax.broadcasted_iota(jnp.int32, s.shape, 1)
            qpos = qi * tq + jnp.where(lane >= tq, lane - tq, lane)
            s = jnp.where(kpos <= qpos, s, NEG_INF)
        m_new = jnp.maximum(m_old, jnp.max(s, axis=0, keepdims=True))
        alpha = jnp.exp2(m_old - m_new)
        p = jnp.exp2(s - m_new)
        l_new = alpha * l_old + jnp.sum(p, axis=0, keepdims=True)
        acc_new = alpha * acc_old + jnp.dot(vb, p.astype(BF16), preferred_element_type=F32)
        return m_new, l_new, acc_new

    init = (jnp.full((1, 2 * tq), NEG_INF, F32), jnp.zeros((1, 2 * tq), F32),
            jnp.zeros((DA_V, 2 * tq), F32))
    visible = qi * tq
    n_big = visible // tk_big
    carry = lax.fori_loop(0, n_big, lambda j, c: update(j * tk_big, tk_big, False, c), init)
    carry = lax.cond(visible - n_big * tk_big >= tk,
                     lambda c: update(n_big * tk_big, tk, False, c), lambda c: c, carry)
    _, l, acc = update(visible, tk, True, carry)

    lv = lam_ref[...]
    la = jnp.sum(lv[0:1] * lv[1:2], axis=-1, keepdims=True)
    lb = jnp.sum(lv[2:3] * lv[3:4], axis=-1, keepdims=True)
    lam = jnp.exp(la) - jnp.exp(lb) + lam_init
    o = acc[:, :tq] / l[:, :tq] - lam * (acc[:, tq:] / l[:, tq:])
    ms = jnp.mean(o * o, axis=0, keepdims=True)
    o = o * lax.rsqrt(ms + EPS) * gs_ref[...] * (1.0 - lam_init)
    o_ref[0] = o.astype(BF16)


def _attention(qt, k, vt, lam_vecs, g_sub, lam_init):
    B, D, S = qt.shape
    tq, tk, tk_big = ATT_TQ, ATT_TK, ATT_TK_BIG
    assert tk == tq and tk_big == 2 * tk and S % tk == 0
    hd = 2 * DA_QK
    return pl.pallas_call(
        functools.partial(_attn_kernel, tq=tq, tk=tk, tk_big=tk_big, lam_init=lam_init),
        out_shape=jax.ShapeDtypeStruct((B, D, S), BF16),
        grid=(B, DA_HEADS, S // tq),
        in_specs=[
            pl.BlockSpec((1, hd, tq), lambda b, h, q: (b, h, q)),
            pl.BlockSpec((1, S, hd), lambda b, h, q: (b, 0, h)),
            pl.BlockSpec((1, DA_V, S), lambda b, h, q: (b, h, 0)),
            pl.BlockSpec((4, DA_QK), lambda b, h, q: (0, 0)),
            pl.BlockSpec((DA_V, 1), lambda b, h, q: (0, 0)),
        ],
        out_specs=pl.BlockSpec((1, DA_V, tq), lambda b, h, q: (b, h, q)),
        compiler_params=_params(("arbitrary", "arbitrary", "arbitrary")),
        name="diff_attention",
    )(qt, k, vt, lam_vecs, g_sub.reshape(DA_V, 1))


def _oproj_kernel(ot_ref, wt_ref, x_ref, mod_ref, o_ref):
    yt = jnp.dot(wt_ref[...], ot_ref[0], preferred_element_type=F32)
    g1 = mod_ref[0, 2:3, :]
    o_ref[0] = x_ref[0] + g1 * yt.T


def _oproj(ot, w_out, x, mod):
    B, S, D = x.shape
    ts = QKV_TS
    return pl.pallas_call(
        _oproj_kernel,
        out_shape=jax.ShapeDtypeStruct((B, S, D), F32),
        grid=(B, S // ts),
        in_specs=[
            pl.BlockSpec((1, D, ts), lambda b, s: (b, 0, s)),
            pl.BlockSpec((D, D), lambda b, s: (0, 0)),
            pl.BlockSpec((1, ts, D), lambda b, s: (b, s, 0)),
            pl.BlockSpec((1, 6, D), lambda b, s: (b, 0, 0)),
        ],
        out_specs=pl.BlockSpec((1, ts, D), lambda b, s: (b, s, 0)),
        compiler_params=_params(("arbitrary", "arbitrary")),
        name="attn_out_proj",
    )(ot, w_out.T.astype(BF16), x, mod)


def _top_pair_sums(a, b, k):
    n = F32_SUBLANES
    assert k == 2 * n and len(a) == k and len(b) == k
    b_lo = jnp.concatenate(b[:n], axis=0)
    lists = [a[p] + b_lo for p in range(n)]
    row0 = a[0] + jnp.concatenate(b[n:], axis=0)
    col0 = jnp.concatenate(a[n:], axis=0) + b[0]
    vals = []
    for _ in range(k):
        head = jnp.maximum(jnp.maximum(lists[0], row0), col0)
        m = jnp.max(head, axis=0, keepdims=True)
        vals.append(m)
        hit = lists[0] == m
        for r in range(n - 1):
            lists[r] = jnp.where(hit, lists[r + 1], lists[r])
        lists[n - 1] = jnp.where(hit, NEG_INF, lists[n - 1])
        row0 = jnp.where(row0 == m, NEG_INF, row0)
        col0 = jnp.where(col0 == m, NEG_INF, col0)
    return vals


def _rank_among(s, b):
    assert len(b) == 16
    lo8 = s >= b[7]
    lo4 = s >= jnp.where(lo8, b[3], b[11])
    lo2 = s >= jnp.where(lo8, jnp.where(lo4, b[1], b[5]), jnp.where(lo4, b[9], b[13]))
    lo1 = s >= jnp.where(
        lo8,
        jnp.where(lo4, jnp.where(lo2, b[0], b[2]), jnp.where(lo2, b[4], b[6])),
        jnp.where(lo4, jnp.where(lo2, b[8], b[10]), jnp.where(lo2, b[12], b[14])))
    rank = (jnp.where(lo8, 0.0, 8.0) + jnp.where(lo4, 0.0, 4.0)
            + jnp.where(lo2, 0.0, 2.0) + jnp.where(lo1, 0.0, 1.0))
    return jnp.where(s >= b[15], rank, 16.0)


def _top_values_128(s, k):
    n = s.shape[0] // 8
    assert n == 16 and k <= n
    blocks = [s[8 * v:8 * v + 8, :] for v in range(n)]
    size = 2
    while size <= n:
        stride = size // 2
        while stride >= 1:
            for i in range(n):
                j = i ^ stride
                if j > i:
                    hi = jnp.maximum(blocks[i], blocks[j])
                    lo = jnp.minimum(blocks[i], blocks[j])
                    blocks[i], blocks[j] = (hi, lo) if (i & size) == 0 else (lo, hi)
            stride //= 2
        size *= 2
    vals = []
    for t in range(k):
        head = blocks[0]
        m = jnp.max(head, axis=0, keepdims=True)
        vals.append(m)
        hit = head == m
        for r in range(k - 1 - t):
            blocks[r] = jnp.where(hit, blocks[r + 1], blocks[r])
    return vals


def _route_kernel(x_ref, mod_ref, gn_ref, wq_ref, sk_ref, ht_ref, cnt_ref, e1_ref, r2_ref, e2_ref):
    x = x_ref[...]
    sh2 = mod_ref[0, 3:4, :]
    sc2 = mod_ref[0, 4:5, :]
    h = _norm_mod(x, gn_ref[...], sc2, sh2)
    ht_ref[...] = h.T.astype(BF16)
    q = jnp.dot(h.astype(BF16), wq_ref[...], preferred_element_type=F32)
    K = PEER_TOPK
    nk = PEER_NKEYS
    for hd in range(PEER_HEADS):
        rs = slice(hd * nk, (hd + 1) * nk)
        s12 = []
        for p in range(2):
            c0 = (hd * 2 + p) * nk
            qhp = q[:, c0:c0 + nk].astype(BF16)
            s12.append(lax.dot_general(sk_ref[hd * 2 + p], qhp, (((1,), (1,)), ((), ())),
                                       preferred_element_type=F32))
        s1, s2 = s12
        a = _top_values_128(s1, K)
        b = _top_values_128(s2, K)
        r2 = _rank_among(s2, b)
        c = _top_pair_sums(a, b, K)
        tau = c[K - 1]
        zsum = jnp.ones_like(c[0])
        for k in range(1, K):
            zsum = zsum + jnp.exp(c[k] - c[0])
        a_all = jnp.concatenate(a, axis=0)
        ck = jnp.zeros(a_all.shape, F32)
        for l in range(K):
            ck = ck + jnp.where((a_all + b[l]) >= tau, 1.0, 0.0)
        cnt = jnp.zeros(s1.shape, F32)
        for k in range(K):
            cnt = jnp.where(s1 == a[k], ck[k:k + 1, :], cnt)
        cnt_ref[rs, :] = cnt
        e1_ref[rs, :] = jnp.exp(s1 - a[0])
        r2_ref[rs, :] = r2.astype(BF16)
        e2_ref[rs, :] = (0.5 * jnp.exp(s2 - b[0]) / zsum).astype(BF16)


def _dense_kernel(ht_ref, u_ref, vt_ref, cnt_ref, e1_ref, r2_ref, e2_ref, x_ref, mod_ref, fn_ref,
                  o_ref, acc_scr, *wa_scrs, sub_size, chunk, n_steps, final_norm):
    ec = pl.program_id(1)
    nk = PEER_NKEYS
    per = sub_size // nk
    group = F32_SUBLANES
    T = ht_ref.shape[1]

    @pl.when(ec == 0)
    def _():
        acc_scr[...] = jnp.zeros(acc_scr.shape, F32)

    def gate_block(i_rel):
        w = jnp.zeros((nk, T), BF16)
        grp, ii = divmod(i_rel, group)
        for hd in range(PEER_HEADS):
            rs = slice(hd * nk, (hd + 1) * nk)
            base = pl.multiple_of(hd * nk + ec * (len(wa_scrs) * per) + grp * group, group)
            cnt_row = cnt_ref[pl.ds(base, group), :][ii:ii + 1, :]
            e1_row = e1_ref[pl.ds(base, group), :][ii:ii + 1, :]
            cb = jnp.broadcast_to(cnt_row, (nk, T)).astype(BF16)
            eb = jnp.broadcast_to(e1_row, (nk, T)).astype(BF16)
            sel = r2_ref[rs, :] < cb
            w = w + jnp.where(sel, e2_ref[rs, :], jnp.zeros((), BF16)) * eb
        return w

    total = None
    for sidx, wa_scr in enumerate(wa_scrs):
        for c in range(per // chunk):
            rows = slice(c * chunk * nk, (c + 1) * chunk * nk)
            urows = slice(sidx * sub_size + c * chunk * nk, sidx * sub_size + (c + 1) * chunk * nk)
            act = jnp.dot(u_ref[urows, :], ht_ref[...], preferred_element_type=F32)
            ab = act.astype(BF16)
            g = ab * (1.0 + lax.erf(ab * 0.7071067811865476))
            w = jnp.concatenate([gate_block(sidx * per + c * chunk + j) for j in range(chunk)],
                                axis=0)
            wa_scr[rows, :] = w * g
        y = jnp.dot(vt_ref[:, sidx * sub_size:(sidx + 1) * sub_size], wa_scr[...],
                    preferred_element_type=F32)
        total = y if total is None else total + y
    acc_scr[...] += total

    @pl.when(ec == n_steps - 1)
    def _():
        g2 = mod_ref[0, 5:6, :]
        y = x_ref[...] + g2 * acc_scr[...].T
        if final_norm:
            ms = jnp.mean(y * y, axis=-1, keepdims=True)
            y = y * lax.rsqrt(ms + EPS) * fn_ref[...]
        o_ref[...] = y


def _peer(x, mod, gn, w_q, sub_keys, u_bf, vt_bf, fn=None):
    B, S, D = x.shape
    N = B * S
    H, NK = PEER_HEADS, PEER_NKEYS
    E = u_bf.shape[0]
    tr = ROUTE_T
    x2 = x.reshape(N, D)
    wq = w_q.reshape(D, H * 2 * NK).astype(BF16)
    sk = sub_keys.reshape(H * 2, NK, NK).astype(BF16)
    ht, cnt, e1, r2, e2 = pl.pallas_call(
        _route_kernel,
        out_shape=(
            jax.ShapeDtypeStruct((D, N), BF16),
            jax.ShapeDtypeStruct((H * NK, N), F32),
            jax.ShapeDtypeStruct((H * NK, N), F32),
            jax.ShapeDtypeStruct((H * NK, N), BF16),
            jax.ShapeDtypeStruct((H * NK, N), BF16),
        ),
        grid=(N // tr,),
        in_specs=[
            pl.BlockSpec((tr, D), lambda i: (i, 0)),
            pl.BlockSpec((1, 6, D), lambda i: ((i * tr) // S, 0, 0)),
            pl.BlockSpec((1, D), lambda i: (0, 0)),
            pl.BlockSpec((D, H * 2 * NK), lambda i: (0, 0)),
            pl.BlockSpec((H * 2, NK, NK), lambda i: (0, 0, 0)),
        ],
        out_specs=(
            pl.BlockSpec((D, tr), lambda i: (0, i)),
            pl.BlockSpec((H * NK, tr), lambda i: (0, i)),
            pl.BlockSpec((H * NK, tr), lambda i: (0, i)),
            pl.BlockSpec((H * NK, tr), lambda i: (0, i)),
            pl.BlockSpec((H * NK, tr), lambda i: (0, i)),
        ),
        compiler_params=_params(("arbitrary",)),
        name="peer_route",
    )(x2, mod, gn.reshape(1, D), wq, sk)

    td, sub, nsub = DENSE_T, DENSE_SUB, DENSE_NSUB
    ec = sub * nsub
    assert (sub // NK) % 8 == 0 and (sub // NK) % DENSE_CHUNK == 0 and E % ec == 0
    assert S % td == 0
    rank_spec = pl.BlockSpec((H * NK, td), lambda t, e: (0, t))
    fn_arr = jnp.ones((1, D), F32) if fn is None else fn.reshape(1, D)
    out = pl.pallas_call(
        functools.partial(_dense_kernel, sub_size=sub, chunk=DENSE_CHUNK, n_steps=E // ec,
                          final_norm=fn is not None),
        out_shape=jax.ShapeDtypeStruct((N, D), F32),
        grid=(N // td, E // ec),
        in_specs=[
            pl.BlockSpec((D, td), lambda t, e: (0, t)),
            pl.BlockSpec((ec, D), lambda t, e: (e, 0)),
            pl.BlockSpec((D, ec), lambda t, e: (0, e)),
            rank_spec, rank_spec, rank_spec, rank_spec,
            pl.BlockSpec((td, D), lambda t, e: (t, 0)),
            pl.BlockSpec((1, 6, D), lambda t, e: ((t * td) // S, 0, 0)),
            pl.BlockSpec((1, D), lambda t, e: (0, 0)),
        ],
        out_specs=pl.BlockSpec((td, D), lambda t, e: (t, 0)),
        scratch_shapes=[pltpu.VMEM((D, td), F32)] + [pltpu.VMEM((sub, td), BF16)] * nsub,
        compiler_params=_params(("arbitrary", "arbitrary")),
        name="peer_experts",
    )(ht, u_bf, vt_bf, cnt, e1, r2, e2, x2, mod, fn_arr)
    return out.reshape(B, S, D)


def kernel(x, c, positions, ada_w, ada_b, norm_mix, norm_ffn, ev_w_in, ev_g_v, ev_w_s, ev_b_s,
           ev_w_pool, ev_pool_scale, ev_w_out, od_w_in, od_lam_q1, od_lam_k1, od_lam_q2,
           od_lam_k2, od_g_sub, od_w_out, peer_w_q, peer_sub_keys, peer_u, peer_v, final_norm):
    B, S, D = x.shape
    depth = ada_w.shape[0]
    inv_freq = 1.0 / (ROPE_THETA ** (jnp.arange(0, DA_QK, 2, dtype=F32) / DA_QK))
    ang = positions.astype(F32)[..., None] * inv_freq
    cos_t = jnp.swapaxes(jnp.cos(ang), 1, 2)
    sin_t = jnp.swapaxes(jnp.sin(ang), 1, 2)
    mods = _ada(c, ada_w, ada_b)
    for l in range(depth):
        mod = mods[l]
        if l % 2 == 0:
            e = l // 2
            x = _even_layer(x, mod, norm_mix[l], ev_w_in[e], ev_g_v[e], ev_w_s[e], ev_b_s[e],
                            ev_w_pool[e], ev_pool_scale[e], ev_w_out[e])
        else:
            o = l // 2
            lam_init = 0.8 - 0.6 * math.exp(-0.3 * l)
            qt, k, vt = _qkv(x, mod, norm_mix[l], od_w_in[o], cos_t, sin_t)
            lam_vecs = jnp.stack([od_lam_q1[o], od_lam_k1[o], od_lam_q2[o], od_lam_k2[o]])
            ot = _attention(qt, k, vt, lam_vecs, od_g_sub[o], lam_init)
            x = _oproj(ot, od_w_out[o], x, mod)
        x = _peer(x, mod, norm_ffn[l], peer_w_q[l], peer_sub_keys[l],
                  peer_u[l].astype(BF16), peer_v[l].T.astype(BF16),
                  fn=final_norm if l == depth - 1 else None)
    return x
```
